```python
import jax, jax.numpy as jnp
from jax import lax
import numpy as np


D_MODEL = 4096
BATCH = 2
SEQ = 4096
DEPTH = 1
DEC_BATCH = 128
DEC_SEQ = 8
PAST_LEN = 2048
PAGE_SIZE = 128

BRANCH_WIDTH = D_MODEL // 2
DH_SB = 128
H_SB = BRANCH_WIDTH // DH_SB
SB_BIAS_INIT = -6.0
DK_RET = 128
DV_RET = 256
H_RET = BRANCH_WIDTH // DV_RET
N_BRANCHES = 2
N_EXPERTS = 32
TOP_K = 4
D_FF = D_MODEL
SWIGLU_LIMIT = 7.0
SWIGLU_ALPHA = 1.702
RMS_EPS = 1e-5
GN_EPS = 1e-6
ROPE_BASE = 10000.0
Q_BLOCK = 128
RET_CHUNK = 128
POOL_NUM = 5
POOL_DEN = 4
IN_WIDTHS = (BRANCH_WIDTH, BRANCH_WIDTH, BRANCH_WIDTH, H_RET * DK_RET, H_RET * DK_RET,
             BRANCH_WIDTH, BRANCH_WIDTH, D_MODEL, D_MODEL)
D_IN = sum(IN_WIDTHS)

kernel_name = 'hybrid_stickbreak_retention_moe_step'


def rms_norm(x, gain):
    xf = x.astype(jnp.float32)
    y = xf * lax.rsqrt(jnp.mean(xf * xf, axis=-1, keepdims=True) + RMS_EPS)
    return (y * gain.astype(jnp.float32)).astype(x.dtype)


def rotary(x, pos):
    half = x.shape[-1] // 2
    inv_freq = ROPE_BASE ** (-jnp.arange(half, dtype=jnp.float32) / half)
    ang = pos.astype(jnp.float32)[:, None] * inv_freq[None, :]
    cos = jnp.cos(ang)[None, :, None, :]
    sin = jnp.sin(ang)[None, :, None, :]
    xf = x.astype(jnp.float32)
    x1, x2 = xf[..., :half], xf[..., half:]
    return jnp.concatenate([x1 * cos - x2 * sin, x1 * sin + x2 * cos], axis=-1).astype(x.dtype)


def stick_breaking(q, pos, k_new, v_new, k_past, v_past, bias):
    B, Sq, H, dh = q.shape
    blk = Q_BLOCK if Sq % Q_BLOCK == 0 else Sq
    nblk = Sq // blk
    scale = dh ** -0.5
    bias_f = bias.astype(jnp.float32)[None, :, None, None]
    qb = q.reshape(B, nblk, blk, H, dh).swapaxes(0, 1)
    pb = pos.reshape(nblk, blk)

    def one_block(args):
        q_i, p_i = args
        z_n = jnp.einsum('bqhd,bkhd->bhqk', q_i, k_new, preferred_element_type=jnp.float32) * scale + bias_f
        visible = pos[None, :] < p_i[:, None]
        sp_n = jnp.where(visible, jax.nn.softplus(z_n), 0.0)
        skip_n = lax.cumsum(sp_n, axis=3, reverse=True) - sp_n
        w_n = jnp.where(visible, jnp.exp(jax.nn.log_sigmoid(z_n) - skip_n), 0.0)
        o = jnp.einsum('bhqk,bkhd->bqhd', w_n.astype(v_new.dtype), v_new,
                       preferred_element_type=jnp.float32)
        if k_past is not None:
            z_p = jnp.einsum('bqhd,bkhd->bhqk', q_i, k_past, preferred_element_type=jnp.float32) * scale + bias_f
            sp_p = jax.nn.softplus(z_p)
            skip_p = (lax.cumsum(sp_p, axis=3, reverse=True) - sp_p
                      + jnp.sum(sp_n, axis=3, keepdims=True))
            w_p = jnp.exp(jax.nn.log_sigmoid(z_p) - skip_p)
            o = o + jnp.einsum('bhqk,bkhd->bqhd', w_p.astype(v_past.dtype), v_past,
                               preferred_element_type=jnp.float32)
        return o

    o = lax.map(one_block, (qb, pb))
    return o.swapaxes(0, 1).reshape(B, Sq, H, v_new.shape[-1]).astype(q.dtype)


def retention(q, k, v, state0):
    B, S, H, dk = q.shape
    dv = v.shape[-1]
    C = RET_CHUNK if S % RET_CHUNK == 0 else S
    n = S // C
    log_gamma = jnp.log(1.0 - 2.0 ** (-5.0 - jnp.arange(H, dtype=jnp.float32)))
    idx = jnp.arange(C, dtype=jnp.float32)
    diff = idx[:, None] - idx[None, :]
    intra = jnp.where(diff >= 0, jnp.exp(log_gamma[:, None, None] * jnp.maximum(diff, 0.0)), 0.0)
    q_decay = jnp.exp(log_gamma[None, :] * (idx[:, None] + 1.0))
    k_decay = jnp.exp(log_gamma[None, :] * (C - 1.0 - idx[:, None]))
    chunk_decay = jnp.exp(log_gamma * C)

    def chunks(t):
        return t.astype(jnp.float32).reshape(B, n, C, H, t.shape[-1]).swapaxes(0, 1)

    def step(state, inp):
        q_c, k_c, v_c = inp
        scores = jnp.einsum('bqhd,bkhd->bhqk', q_c, k_c) * intra[None]
        inner = jnp.einsum('bhqk,bkhv->bqhv', scores, v_c)
        cross = jnp.einsum('bqhd,bhdv->bqhv', q_c, state) * q_decay[None, :, :, None]
        new_state = (state * chunk_decay[None, :, None, None]
                     + jnp.einsum('bkhd,bkhv->bhdv', k_c * k_decay[None, :, :, None], v_c))
        return new_state, inner + cross

    state, o = lax.scan(step, state0.astype(jnp.float32), (chunks(q), chunks(k), chunks(v)))
    return o.swapaxes(0, 1).reshape(B, S, H, dv), state


def head_group_norm(o, gain):
    mu = jnp.mean(o, axis=-1, keepdims=True)
    var = jnp.mean(jnp.square(o - mu), axis=-1, keepdims=True)
    return (o - mu) * lax.rsqrt(var + GN_EPS) * gain.astype(jnp.float32)[None, None]


def moe(x, w_router, b_router, w_gate_up, b_gate_up, w_down, b_down):
    B, S, D = x.shape
    xt = x.reshape(B * S, D)
    logits = (xt @ w_router + b_router).astype(jnp.float32)
    top_val, top_idx = lax.top_k(logits, TOP_K)
    top_w = jax.nn.softmax(top_val, axis=-1)
    combine = jnp.einsum('tk,tke->te', top_w, jax.nn.one_hot(top_idx, N_EXPERTS, dtype=jnp.float32))
    out = jnp.zeros((B * S, D), jnp.float32)
    for e in range(N_EXPERTS):
        gu = xt @ w_gate_up[e] + b_gate_up[e]
        glu = jnp.minimum(gu[:, 0::2], SWIGLU_LIMIT)
        lin = jnp.clip(gu[:, 1::2], -SWIGLU_LIMIT, SWIGLU_LIMIT)
        hdn = glu * jax.nn.sigmoid(SWIGLU_ALPHA * glu) * (lin + 1.0)
        out = out + combine[:, e:e + 1] * (hdn @ w_down[e] + b_down[e])
    return out.reshape(B, S, D).astype(x.dtype)


def hybrid_layer(x, pos, k_past, v_past, ret_state0, norm_mix_gain, w_in, sb_bias, ret_gn_gain, w_branch,
                 w_out, norm_ffn_gain, w_router, b_router, w_gate_up, b_gate_up, w_down, b_down):
    B, S, _ = x.shape
    xn = rms_norm(x, norm_mix_gain)
    proj = jnp.einsum('bsd,de->bse', xn, w_in)
    split_at = [int(c) for c in np.cumsum(IN_WIDTHS)[:-1]]
    q_sb, k_sb, v_sb, q_r, k_r, v_r, g_r, gate_a, gate_b = jnp.split(proj, split_at, axis=-1)
    q_sb = q_sb.reshape(B, S, H_SB, DH_SB)
    k_sb = k_sb.reshape(B, S, H_SB, DH_SB)
    v_sb = v_sb.reshape(B, S, H_SB, DH_SB)
    o_sb = stick_breaking(q_sb, pos, k_sb, v_sb, k_past, v_past, sb_bias).reshape(B, S, BRANCH_WIDTH)
    q_r = rotary(q_r.reshape(B, S, H_RET, DK_RET), pos)
    k_r = rotary(k_r.reshape(B, S, H_RET, DK_RET), pos) * (DK_RET ** -0.5)
    o_r, ret_state = retention(q_r, k_r, v_r.reshape(B, S, H_RET, DV_RET), ret_state0)
    o_r = head_group_norm(o_r, ret_gn_gain).reshape(B, S, BRANCH_WIDTH)
    o_r = (jax.nn.silu(g_r.astype(jnp.float32)) * o_r).astype(x.dtype)
    branches = jnp.stack([o_sb, o_r], axis=2)
    gates = jax.nn.sigmoid(jnp.stack([gate_a, gate_b], axis=2))
    merged = jnp.sum(gates * jnp.einsum('bsnw,nwd->bsnd', branches, w_branch), axis=2)
    h = x + jnp.einsum('bsd,de->bse', merged, w_out)
    y = h + moe(rms_norm(h, norm_ffn_gain), w_router, b_router, w_gate_up, b_gate_up, w_down, b_down)
    return y, k_sb, v_sb, ret_state


def setup_inputs(seed: int = 0) -> dict:
    key = jax.random.key(seed)
    ks = jax.random.split(key, 24)
    n_pages = PAST_LEN // PAGE_SIZE
    n_pool = (DEC_BATCH * n_pages * POOL_NUM) // POOL_DEN
    f32 = jnp.float32
    nrm = lambda k, shape, s: jax.random.normal(k, shape, f32) * s
    page_table = jax.random.permutation(ks[5], n_pool)[:DEC_BATCH * n_pages]
    page_table = page_table.reshape(DEC_BATCH, n_pages).astype(jnp.int32)
    return {
        'x_prompt': nrm(ks[0], (BATCH, SEQ, D_MODEL), 1.0),
        'x_sample': nrm(ks[1], (DEC_BATCH, DEC_SEQ, D_MODEL), 1.0),
        'cache_k': nrm(ks[2], (DEPTH, n_pool, PAGE_SIZE, H_SB, DH_SB), 1.0),
        'cache_v': nrm(ks[3], (DEPTH, n_pool, PAGE_SIZE, H_SB, DH_SB), 1.0),
        'state_ret': nrm(ks[4], (DEPTH, DEC_BATCH, H_RET, DK_RET, DV_RET), 1.0),
        'page_table': page_table,
        'norm_mix_gain': 1.0 + nrm(ks[6], (DEPTH, D_MODEL), 0.02),
        'w_in': nrm(ks[7], (DEPTH, D_MODEL, D_IN), D_MODEL ** -0.5),
        'sb_bias': SB_BIAS_INIT + nrm(ks[19], (DEPTH, H_SB), 0.1),
        'ret_gn_gain': 1.0 + nrm(ks[8], (DEPTH, H_RET, DV_RET), 0.02),
        'w_branch': nrm(ks[9], (DEPTH, N_BRANCHES, BRANCH_WIDTH, D_MODEL), BRANCH_WIDTH ** -0.5),
        'w_out': nrm(ks[10], (DEPTH, D_MODEL, D_MODEL), D_MODEL ** -0.5),
        'norm_ffn_gain': 1.0 + nrm(ks[11], (DEPTH, D_MODEL), 0.02),
        'w_router': nrm(ks[12], (DEPTH, D_MODEL, N_EXPERTS), D_MODEL ** -0.5),
        'b_router': nrm(ks[13], (DEPTH, N_EXPERTS), 0.01),
        'w_gate_up': nrm(ks[14], (DEPTH, N_EXPERTS, D_MODEL, 2 * D_FF), D_MODEL ** -0.5),
        'b_gate_up': nrm(ks[15], (DEPTH, N_EXPERTS, 2 * D_FF), 0.01),
        'w_down': nrm(ks[16], (DEPTH, N_EXPERTS, D_FF, D_MODEL), D_FF ** -0.5),
        'b_down': nrm(ks[17], (DEPTH, N_EXPERTS, D_MODEL), 0.01),
        'norm_final_gain': 1.0 + nrm(ks[18], (D_MODEL,), 0.02),
    }


def reference(x_prompt, x_sample, cache_k, cache_v, state_ret, page_table, norm_mix_gain, w_in, sb_bias,
              ret_gn_gain, w_branch, w_out, norm_ffn_gain, w_router, b_router, w_gate_up, b_gate_up,
              w_down, b_down, norm_final_gain):
    n_seq_p, seq_p = x_prompt.shape[0], x_prompt.shape[1]
    n_seq_s, seq_s = x_sample.shape[0], x_sample.shape[1]
    past_len = page_table.shape[1] * cache_k.shape[2]
    pos_prompt = jnp.arange(seq_p, dtype=jnp.int32)
    pos_sample = past_len + jnp.arange(seq_s, dtype=jnp.int32)
    zero_state = jnp.zeros((n_seq_p, H_RET, DK_RET, DV_RET), jnp.float32)
    hp, hs = x_prompt, x_sample
    kp_l, vp_l, sp_l, ks_l, vs_l, ss_l = [], [], [], [], [], []
    for layer in range(DEPTH):
        w = (norm_mix_gain[layer], w_in[layer], sb_bias[layer], ret_gn_gain[layer], w_branch[layer],
             w_out[layer], norm_ffn_gain[layer], w_router[layer], b_router[layer], w_gate_up[layer],
             b_gate_up[layer], w_down[layer], b_down[layer])
        hp, kp, vp, sp = hybrid_layer(hp, pos_prompt, None, None, zero_state, *w)
        k_past = cache_k[layer][page_table].reshape(n_seq_s, past_len, H_SB, DH_SB)
        v_past = cache_v[layer][page_table].reshape(n_seq_s, past_len, H_SB, DH_SB)
        hs, kn, vn, sn = hybrid_layer(hs, pos_sample, k_past, v_past, state_ret[layer], *w)
        kp_l.append(kp); vp_l.append(vp); sp_l.append(sp)
        ks_l.append(kn); vs_l.append(vn); ss_l.append(sn)
    y_prompt = rms_norm(hp, norm_final_gain)
    y_sample = rms_norm(hs, norm_final_gain)
    return (y_prompt, y_sample, jnp.stack(kp_l), jnp.stack(vp_l), jnp.stack(sp_l),
            jnp.stack(ks_l), jnp.stack(vs_l), jnp.stack(ss_l))
```

```python
import functools
import math

import jax
import jax.numpy as jnp
from jax import lax
from jax.experimental import pallas as pl
from jax.experimental.pallas import tpu as pltpu

TOP_K = 4
SWIGLU_LIMIT = 7.0
SWIGLU_ALPHA = 1.702
RMS_EPS = 1e-5
GN_EPS = 1e-6
ROPE_BASE = 10000.0
RET_CHUNK = 128

LANES = 128
MXU_DIM = 256
MIB = 1024 * 1024
HIGHEST = lax.Precision.HIGHEST
F32 = jnp.float32
BF16 = jnp.bfloat16
NEG_BIG = -1e30


def _pick(n, pref, mult=8):
    t = min(pref, n)
    t -= t % mult
    while t >= mult:
        if n % t == 0:
            return t
        t -= mult
    return n


def _params(n_grid, vmem_mib):
    return pltpu.CompilerParams(dimension_semantics=("arbitrary",) * n_grid,
                                vmem_limit_bytes=int(vmem_mib * MIB))


def _sigmoid(x):
    return 1.0 / (1.0 + jnp.exp(-x))


def _softplus(z):
    return jnp.maximum(z, 0.0) + jnp.log(1.0 + jnp.exp(-jnp.abs(z)))


def _dot_nt(a, b, **kw):
    return lax.dot_general(a, b, (((1,), (1,)), ((), ())), preferred_element_type=F32, **kw)


def _dot_tn(a, b, **kw):
    return lax.dot_general(a, b, (((0,), (0,)), ((), ())), preferred_element_type=F32, **kw)


def _rmsnorm_body(x_ref, g_ref, o_ref):
    x = x_ref[...]
    ms = jnp.mean(x * x, axis=-1, keepdims=True)
    o_ref[...] = (x * lax.rsqrt(ms + RMS_EPS) * g_ref[...]).astype(o_ref.dtype)


def _rmsnorm(x, gain, out_dtype):
    t, d = x.shape
    tm = _pick(t, 256)
    return pl.pallas_call(
        _rmsnorm_body,
        grid=(t // tm,),
        in_specs=[pl.BlockSpec((tm, d), lambda i: (i, 0)),
                  pl.BlockSpec((1, d), lambda i: (0, 0))],
        out_specs=pl.BlockSpec((tm, d), lambda i: (i, 0)),
        out_shape=jax.ShapeDtypeStruct((t, d), out_dtype),
        compiler_params=_params(1, 32),
        name="rmsnorm",
    )(x, gain.reshape(1, d))


def _mm_body(a_ref, w_ref, o_ref, wb_ref):
    @pl.when(pl.program_id(1) == 0)
    def _():
        wb_ref[...] = w_ref[...].astype(BF16)

    o_ref[...] = jnp.dot(a_ref[...], wb_ref[...], preferred_element_type=F32)


def _mm_res_body(a_ref, w_ref, r_ref, o_ref, wb_ref):
    @pl.when(pl.program_id(1) == 0)
    def _():
        wb_ref[...] = w_ref[...].astype(BF16)

    o_ref[...] = r_ref[...] + jnp.dot(a_ref[...], wb_ref[...], preferred_element_type=F32)


def _matmul(a, w, residual=None, name="matmul"):
    m, k = a.shape
    n = w.shape[1]
    tm, tn = _pick(m, 512), _pick(n, 512, LANES)
    in_specs = [pl.BlockSpec((tm, k), lambda j, i: (i, 0)),
                pl.BlockSpec((k, tn), lambda j, i: (0, j))]
    args = [a, w]
    body = _mm_body
    if residual is not None:
        in_specs.append(pl.BlockSpec((tm, tn), lambda j, i: (i, j)))
        args.append(residual)
        body = _mm_res_body
    return pl.pallas_call(
        body,
        grid=(n // tn, m // tm),
        in_specs=in_specs,
        out_specs=pl.BlockSpec((tm, tn), lambda j, i: (i, j)),
        out_shape=jax.ShapeDtypeStruct((m, n), F32),
        scratch_shapes=[pltpu.VMEM((k, tn), BF16)],
        compiler_params=_params(2, 44),
        name=name,
    )(*args)


def _branch_body(a0_ref, a1_ref, w0_ref, w1_ref, ga_ref, gb_ref, o_ref, w0b_ref, w1b_ref):
    @pl.when(pl.program_id(1) == 0)
    def _():
        w0b_ref[...] = w0_ref[...].astype(BF16)
        w1b_ref[...] = w1_ref[...].astype(BF16)

    p0 = jnp.dot(a0_ref[...], w0b_ref[...], preferred_element_type=F32)
    p1 = jnp.dot(a1_ref[...], w1b_ref[...], preferred_element_type=F32)
    o_ref[...] = (_sigmoid(ga_ref[...]) * p0 + _sigmoid(gb_ref[...]) * p1).astype(o_ref.dtype)


def _branch_merge(o_sb, o_r, w_branch, proj, d_model):
    t, w = o_sb.shape
    tm, tn = _pick(t, 512), _pick(d_model, 512, LANES)
    ga_blk = 3 * d_model // tn
    gb_blk = 4 * d_model // tn
    return pl.pallas_call(
        _branch_body,
        grid=(d_model // tn, t // tm),
        in_specs=[pl.BlockSpec((tm, w), lambda j, i: (i, 0)),
                  pl.BlockSpec((tm, w), lambda j, i: (i, 0)),
                  pl.BlockSpec((None, w, tn), lambda j, i: (0, 0, j)),
                  pl.BlockSpec((None, w, tn), lambda j, i: (1, 0, j)),
                  pl.BlockSpec((tm, tn), lambda j, i: (i, ga_blk + j)),
                  pl.BlockSpec((tm, tn), lambda j, i: (i, gb_blk + j))],
        out_specs=pl.BlockSpec((tm, tn), lambda j, i: (i, j)),
        out_shape=jax.ShapeDtypeStruct((t, d_model), BF16),
        scratch_shapes=[pltpu.VMEM((w, tn), BF16), pltpu.VMEM((w, tn), BF16)],
        compiler_params=_params(2, 44),
        name="branch_merge",
    )(o_sb, o_r, w_branch, w_branch, proj, proj)


def _reverse_cumsum(sp, tri):
    hi = sp.astype(BF16)
    lo = (sp - hi.astype(F32)).astype(BF16)
    return (jnp.dot(hi, tri, preferred_element_type=F32) + jnp.dot(lo, tri, preferred_element_type=F32))


def _sb_prompt_body(qi_ref, kb_ref, bias_ref, q_ref, k_ref, v_ref, tri_ref, o_ref, acc_ref, carry_ref,
                    *, blk, sub, scale):
    h = pl.program_id(1)
    p = pl.program_id(2)
    qi = qi_ref[p]
    kb = kb_ref[p]

    @pl.when(kb == qi)
    def _():
        acc_ref[...] = jnp.zeros_like(acc_ref)
        carry_ref[...] = jnp.zeros_like(carry_ref)

    q = q_ref[...].astype(BF16)
    k = k_ref[...].astype(BF16)
    z = _dot_nt(q, k) * scale + bias_ref[h]
    row = qi * blk + lax.broadcasted_iota(jnp.int32, (blk, blk), 0)
    col = kb * blk + lax.broadcasted_iota(jnp.int32, (blk, blk), 1)
    visible = col < row
    sp = jnp.where(visible, _softplus(z), 0.0)
    tri = tri_ref[...]
    carry = carry_ref[...]
    weights = [None] * (blk // sub)
    for c in reversed(range(blk // sub)):
        cols = slice(c * sub, (c + 1) * sub)
        cs = _reverse_cumsum(sp[:, cols], tri)
        arg = jnp.where(visible[:, cols], z[:, cols] - cs - carry, NEG_BIG)
        weights[c] = jnp.exp(arg).astype(BF16)
        carry = carry + cs[:, 0:1]
    carry_ref[...] = carry
    w = jnp.concatenate(weights, axis=1)
    acc_ref[...] += jnp.dot(w, v_ref[...].astype(BF16), preferred_element_type=F32)

    @pl.when(kb == 0)
    def _():
        o_ref[...] = acc_ref[...].astype(o_ref.dtype)


def _sb_prompt(proj, sb_bias, n_seq, seq, n_heads, dh):
    blk = _pick(seq, 512, LANES)
    sub = _pick(blk, MXU_DIM, LANES)
    nq = seq // blk
    pairs = [(qi, kb) for qi in range(nq) for kb in range(qi, -1, -1)]
    qi_tab = jnp.array([p[0] for p in pairs], jnp.int32)
    kb_tab = jnp.array([p[1] for p in pairs], jnp.int32)
    tri = jnp.tril(jnp.ones((sub, sub), F32)).astype(BF16)
    grid_spec = pltpu.PrefetchScalarGridSpec(
        num_scalar_prefetch=2,
        grid=(n_seq, n_heads, len(pairs)),
        in_specs=[pl.BlockSpec(memory_space=pltpu.SMEM),
                  pl.BlockSpec((blk, dh), lambda b, h, p, qi, kb: (b * nq + qi[p], h)),
                  pl.BlockSpec((blk, dh), lambda b, h, p, qi, kb: (b * nq + kb[p], n_heads + h)),
                  pl.BlockSpec((blk, dh), lambda b, h, p, qi, kb: (b * nq + kb[p], 2 * n_heads + h)),
                  pl.BlockSpec((sub, sub), lambda b, h, p, qi, kb: (0, 0))],
        out_specs=pl.BlockSpec((blk, dh), lambda b, h, p, qi, kb: (b * nq + qi[p], h)),
        scratch_shapes=[pltpu.VMEM((blk, dh), F32), pltpu.VMEM((blk, 1), F32)],
    )
    return pl.pallas_call(
        functools.partial(_sb_prompt_body, blk=blk, sub=sub, scale=dh ** -0.5),
        grid_spec=grid_spec,
        out_shape=jax.ShapeDtypeStruct((n_seq * seq, n_heads * dh), BF16),
        compiler_params=_params(3, 40),
        name="sb_prompt",
    )(qi_tab, kb_tab, sb_bias.astype(F32), proj, proj, proj, tri)


def _sb_sample_body(pt_ref, q_ref, kn_ref, vn_ref, kp_ref, vp_ref, bias_ref, tri_ref, o_ref,
                    acc_ref, carry_ref, *, n_heads, dh, n_new, page, scale):
    p = pl.program_id(1)
    rows = n_heads * n_new
    bias = bias_ref[...]
    tri = tri_ref[...]

    def attend(get_k, get_v, masked):
        z = jnp.concatenate(
            [_dot_nt(q_ref[:, h * dh:(h + 1) * dh].astype(BF16), get_k(h)) for h in range(n_heads)], axis=0)
        z = z * scale + bias
        sp = _softplus(z)
        if masked:
            t = lax.broadcasted_iota(jnp.int32, (rows, page), 0) % n_new
            s = lax.broadcasted_iota(jnp.int32, (rows, page), 1)
            visible = s < t
            sp = jnp.where(visible, sp, 0.0)
        cs = _reverse_cumsum(sp, tri)
        arg = z - cs - carry_ref[...]
        if masked:
            arg = jnp.where(visible, arg, NEG_BIG)
        w = jnp.exp(arg).astype(BF16)
        carry_ref[...] += cs[:, 0:1]
        for h in range(n_heads):
            r = slice(h * n_new, (h + 1) * n_new)
            acc_ref[r, :] += jnp.dot(w[r, :], get_v(h), preferred_element_type=F32)

    @pl.when(p == 0)
    def _():
        acc_ref[...] = jnp.zeros_like(acc_ref)
        carry_ref[...] = jnp.zeros_like(carry_ref)
        pad = jnp.zeros((page - n_new, dh), BF16)

        def new_rows(ref, h):
            return jnp.concatenate([ref[:, h * dh:(h + 1) * dh].astype(BF16), pad], axis=0)

        attend(functools.partial(new_rows, kn_ref), functools.partial(new_rows, vn_ref), True)

    @pl.when(p > 0)
    def _():
        def past_rows(ref, h):
            return ref[pl.ds(h, page, stride=n_heads), :].astype(BF16)

        attend(functools.partial(past_rows, kp_ref), functools.partial(past_rows, vp_ref), False)

    @pl.when(p == pl.num_programs(1) - 1)
    def _():
        for h in range(n_heads):
            o_ref[:, h * dh:(h + 1) * dh] = acc_ref[h * n_new:(h + 1) * n_new, :].astype(o_ref.dtype)


def _sb_sample(proj, row0, cache_k, cache_v, page_table, sb_bias, n_seq, n_new, n_heads, dh):
    n_pool, page = cache_k.shape[0], cache_k.shape[1]
    n_pages = page_table.shape[1]
    width = n_heads * dh
    assert row0 % n_new == 0 and n_new % 8 == 0 and page == LANES
    blk0 = row0 // n_new
    kc = cache_k.reshape(n_pool, page * n_heads, dh)
    vc = cache_v.reshape(n_pool, page * n_heads, dh)
    bias_rows = jnp.repeat(sb_bias.astype(F32), n_new).reshape(n_heads * n_new, 1)
    tri = jnp.tril(jnp.ones((page, page), F32)).astype(BF16)

    def page_idx(b, p, pt):
        return (pt[b * n_pages + jnp.maximum(n_pages - p, 0) - jnp.where(p == 0, 1, 0)], 0, 0)

    grid_spec = pltpu.PrefetchScalarGridSpec(
        num_scalar_prefetch=1,
        grid=(n_seq, n_pages + 1),
        in_specs=[pl.BlockSpec((n_new, width), lambda b, p, pt: (blk0 + b, 0)),
                  pl.BlockSpec((n_new, width), lambda b, p, pt: (blk0 + b, 1)),
                  pl.BlockSpec((n_new, width), lambda b, p, pt: (blk0 + b, 2)),
                  pl.BlockSpec((None, page * n_heads, dh), page_idx),
                  pl.BlockSpec((None, page * n_heads, dh), page_idx),
                  pl.BlockSpec((n_heads * n_new, 1), lambda b, p, pt: (0, 0)),
                  pl.BlockSpec((page, page), lambda b, p, pt: (0, 0))],
        out_specs=pl.BlockSpec((n_new, width), lambda b, p, pt: (b, 0)),
        scratch_shapes=[pltpu.VMEM((n_heads * n_new, dh), F32), pltpu.VMEM((n_heads * n_new, 1), F32)],
    )
    return pl.pallas_call(
        functools.partial(_sb_sample_body, n_heads=n_heads, dh=dh, n_new=n_new, page=page, scale=dh ** -0.5),
        grid_spec=grid_spec,
        out_shape=jax.ShapeDtypeStruct((n_seq * n_new, width), BF16),
        compiler_params=_params(2, 32),
        name="sb_sample",
    )(page_table.reshape(-1).astype(jnp.int32), proj, proj, proj, kc, vc, bias_rows, tri)


def _retention_body(q_ref, k_ref, v_ref, g_ref, cos_ref, sin_ref, gain_ref, s0_ref, o_ref, sout_ref,
                    state_ref, *, n_heads, dk, dv, chunk):
    c = pl.program_id(1)

    @pl.when(c == 0)
    def _():
        state_ref[...] = s0_ref[...]

    cosf = cos_ref[...]
    sinf = sin_ref[...]
    ri = lax.broadcasted_iota(jnp.int32, (chunk, chunk), 0)
    ci = lax.broadcasted_iota(jnp.int32, (chunk, chunk), 1)
    diff = (ri - ci).astype(F32)
    idx = lax.broadcasted_iota(jnp.int32, (chunk, 1), 0).astype(F32)
    for h in range(n_heads):
        log_gamma = math.log(1.0 - 2.0 ** (-5.0 - h))
        q = q_ref[:, h * dk:(h + 1) * dk]
        k = k_ref[:, h * dk:(h + 1) * dk]
        v = v_ref[:, h * dv:(h + 1) * dv]
        q = q * cosf + pltpu.roll(q, dk // 2, axis=1) * sinf
        k = (k * cosf + pltpu.roll(k, dk // 2, axis=1) * sinf) * (dk ** -0.5)
        state = state_ref[h]
        intra = jnp.where(diff >= 0, jnp.exp(log_gamma * jnp.maximum(diff, 0.0)), 0.0)
        scores = _dot_nt(q, k, precision=HIGHEST) * intra
        inner = jnp.dot(scores, v, precision=HIGHEST, preferred_element_type=F32)
        cross = jnp.dot(q, state, precision=HIGHEST, preferred_element_type=F32) * jnp.exp(log_gamma * (idx + 1.0))
        k_dec = k * jnp.exp(log_gamma * (chunk - 1.0 - idx))
        state_ref[h] = state * math.exp(log_gamma * chunk) + _dot_tn(k_dec, v, precision=HIGHEST)
        o = inner + cross
        mu = jnp.mean(o, axis=-1, keepdims=True)
        var = jnp.mean(jnp.square(o - mu), axis=-1, keepdims=True)
        o = (o - mu) * lax.rsqrt(var + GN_EPS) * gain_ref[h:h + 1, :]
        g = g_ref[:, h * dv:(h + 1) * dv]
        o_ref[:, h * dv:(h + 1) * dv] = (g * _sigmoid(g) * o).astype(o_ref.dtype)

    @pl.when(c == pl.num_programs(1) - 1)
    def _():
        sout_ref[...] = state_ref[...]


def _retention(proj, row0, n_seq, seq, pos0, state0, gn_gain, width):
    n_heads, dk, dv = state0.shape[1], state0.shape[2], state0.shape[3]
    chunk = RET_CHUNK if seq % RET_CHUNK == 0 else seq
    n_chunks = seq // chunk
    qk_w = n_heads * dk
    assert n_heads * dv == width and 2 * qk_w == width and row0 % chunk == 0
    blk0 = row0 // chunk
    half = dk // 2
    inv_freq = ROPE_BASE ** (-jnp.arange(half, dtype=F32) / half)
    ang = (pos0 + jnp.arange(seq, dtype=jnp.int32)).astype(F32)[:, None] * inv_freq[None, :]
    cos_t = jnp.concatenate([jnp.cos(ang), jnp.cos(ang)], axis=1)
    sin_t = jnp.concatenate([-jnp.sin(ang), jnp.sin(ang)], axis=1)
    row = lambda b, c: blk0 + b * n_chunks + c
    return pl.pallas_call(
        functools.partial(_retention_body, n_heads=n_heads, dk=dk, dv=dv, chunk=chunk),
        grid=(n_seq, n_chunks),
        in_specs=[pl.BlockSpec((chunk, qk_w), lambda b, c: (row(b, c), 6)),
                  pl.BlockSpec((chunk, qk_w), lambda b, c: (row(b, c), 7)),
                  pl.BlockSpec((chunk, width), lambda b, c: (row(b, c), 4)),
                  pl.BlockSpec((chunk, width), lambda b, c: (row(b, c), 5)),
                  pl.BlockSpec((chunk, dk), lambda b, c: (c, 0)),
                  pl.BlockSpec((chunk, dk), lambda b, c: (c, 0)),
                  pl.BlockSpec((n_heads, dv), lambda b, c: (0, 0)),
                  pl.BlockSpec((None, n_heads, dk, dv), lambda b, c: (b, 0, 0, 0))],
        out_specs=[pl.BlockSpec((chunk, width), lambda b, c: (b * n_chunks + c, 0)),
                   pl.BlockSpec((None, n_heads, dk, dv), lambda b, c: (b, 0, 0, 0))],
        out_shape=[jax.ShapeDtypeStruct((n_seq * seq, width), BF16),
                   jax.ShapeDtypeStruct(state0.shape, F32)],
        scratch_shapes=[pltpu.VMEM((n_heads, dk, dv), F32)],
        compiler_params=_params(2, 32),
        name="retention",
    )(proj, proj, proj, proj, cos_t, sin_t, gn_gain.astype(F32), state0.astype(F32))


def _router_body(h_ref, g_ref, wr_ref, br_ref, xn_ref, idx_ref, wt_ref):
    x = h_ref[...]
    ms = jnp.mean(x * x, axis=-1, keepdims=True)
    xn = x * lax.rsqrt(ms + RMS_EPS) * g_ref[...]
    xn_ref[...] = xn
    logits = jnp.dot(xn, wr_ref[...], precision=HIGHEST, preferred_element_type=F32) + br_ref[...]
    lane = lax.broadcasted_iota(jnp.int32, logits.shape, 1)
    vals, idxs = [], []
    work = logits
    for _ in range(TOP_K):
        m = jnp.max(work, axis=-1, keepdims=True)
        i = jnp.min(jnp.where(work == m, lane, LANES), axis=-1, keepdims=True)
        vals.append(m)
        idxs.append(i)
        work = jnp.where(lane == i, -jnp.inf, work)
    exps = [jnp.exp(v - vals[0]) for v in vals]
    denom = exps[0] + exps[1] + exps[2] + exps[3]
    idx_out = jnp.zeros(logits.shape, jnp.int32)
    wt_out = jnp.zeros(logits.shape, F32)
    for k in range(TOP_K):
        idx_out = jnp.where(lane == k, idxs[k], idx_out)
        wt_out = jnp.where(lane == k, exps[k] / denom, wt_out)
    idx_ref[...] = idx_out
    wt_ref[...] = wt_out


def _router(h, gain, w_router, b_router):
    t, d = h.shape
    e = w_router.shape[1]
    assert e <= LANES
    tm = _pick(t, 256)
    wr = jnp.zeros((d, LANES), F32).at[:, :e].set(w_router)
    br = jnp.full((1, LANES), NEG_BIG, F32).at[0, :e].set(b_router)
    return pl.pallas_call(
        _router_body,
        grid=(t // tm,),
        in_specs=[pl.BlockSpec((tm, d), lambda i: (i, 0)),
                  pl.BlockSpec((1, d), lambda i: (0, 0)),
                  pl.BlockSpec((d, LANES), lambda i: (0, 0)),
                  pl.BlockSpec((1, LANES), lambda i: (0, 0))],
        out_specs=[pl.BlockSpec((tm, d), lambda i: (i, 0)),
                   pl.BlockSpec((tm, LANES), lambda i: (i, 0)),
                   pl.BlockSpec((tm, LANES), lambda i: (i, 0))],
        out_shape=[jax.ShapeDtypeStruct((t, d), F32),
                   jax.ShapeDtypeStruct((t, LANES), jnp.int32),
                   jax.ShapeDtypeStruct((t, LANES), F32)],
        compiler_params=_params(1, 40),
        name="router",
    )(h, gain.reshape(1, d), wr, br)


def _row_copy(src_hbm, src_row, dst, dst_row, sem):
    return pltpu.make_async_copy(src_hbm.at[pl.ds(src_row, 1)], dst.at[pl.ds(dst_row, 1)], sem)


def _dispatch_body(src_ref, x_hbm, o_ref, buf_ref, sem, *, rows):
    base = pl.program_id(0) * rows

    def issue(r, carry):
        _row_copy(x_hbm, src_ref[base + r], buf_ref, r, sem).start()
        return carry

    lax.fori_loop(0, rows, issue, 0)

    def drain(r, carry):
        _row_copy(x_hbm, 0, buf_ref, r, sem).wait()
        return carry

    lax.fori_loop(0, rows, drain, 0)
    o_ref[...] = buf_ref[...].astype(o_ref.dtype)


def _dispatch(xn, src_tok, rows_pad):
    d = xn.shape[1]
    rows = _pick(rows_pad, 256)
    grid_spec = pltpu.PrefetchScalarGridSpec(
        num_scalar_prefetch=1,
        grid=(rows_pad // rows,),
        in_specs=[pl.BlockSpec(memory_space=pl.ANY)],
        out_specs=pl.BlockSpec((rows, d), lambda i, s: (i, 0)),
        scratch_shapes=[pltpu.VMEM((rows, d), F32), pltpu.SemaphoreType.DMA(())],
    )
    return pl.pallas_call(
        functools.partial(_dispatch_body, rows=rows),
        grid_spec=grid_spec,
        out_shape=jax.ShapeDtypeStruct((rows_pad, d), BF16),
        compiler_params=_params(1, 32),
        name="moe_dispatch",
    )(src_tok, xn)


def _combine_body(pos_ref, h_ref, ys_hbm, g_ref, o_ref, buf_ref, sem, *, rows, row0, final_norm):
    base = (row0 + pl.program_id(0) * rows) * TOP_K

    def issue(r, carry):
        for k in range(TOP_K):
            _row_copy(ys_hbm, pos_ref[base + r * TOP_K + k], buf_ref.at[k], r, sem).start()
        return carry

    lax.fori_loop(0, rows, issue, 0)

    def drain(r, carry):
        for k in range(TOP_K):
            _row_copy(ys_hbm, 0, buf_ref.at[k], r, sem).wait()
        return carry

    lax.fori_loop(0, rows, drain, 0)
    y = h_ref[...] + ((buf_ref[0] + buf_ref[1]) + (buf_ref[2] + buf_ref[3]))
    if final_norm:
        ms = jnp.mean(y * y, axis=-1, keepdims=True)
        y = y * lax.rsqrt(ms + RMS_EPS) * g_ref[...]
    o_ref[...] = y


def _combine(h, ys, pos, gain, row0, n_rows, final_norm):
    d = h.shape[1]
    rows = _pick(n_rows, 64)
    assert row0 % rows == 0
    blk0 = row0 // rows
    grid_spec = pltpu.PrefetchScalarGridSpec(
        num_scalar_prefetch=1,
        grid=(n_rows // rows,),
        in_specs=[pl.BlockSpec((rows, d), lambda i, s: (blk0 + i, 0)),
                  pl.BlockSpec(memory_space=pl.ANY),
                  pl.BlockSpec((1, d), lambda i, s: (0, 0))],
        out_specs=pl.BlockSpec((rows, d), lambda i, s: (i, 0)),
        scratch_shapes=[pltpu.VMEM((TOP_K, rows, d), F32), pltpu.SemaphoreType.DMA(())],
    )
    return pl.pallas_call(
        functools.partial(_combine_body, rows=rows, row0=row0, final_norm=final_norm),
        grid_spec=grid_spec,
        out_shape=jax.ShapeDtypeStruct((n_rows, d), F32),
        compiler_params=_params(1, 32),
        name="moe_combine",
    )(pos, h, ys, gain.reshape(1, d))


def _expert_up_body(te_ref, valid_ref, first_ref, x_ref, w_ref, b_ref, sel_ref, o_ref, wb_ref, *, tn):
    i = pl.program_id(1)

    @pl.when(first_ref[i] == 1)
    def _():
        wb_ref[...] = w_ref[...].astype(BF16)

    @pl.when(valid_ref[i] == 1)
    def _():
        gu = jnp.dot(x_ref[...], wb_ref[...], preferred_element_type=F32) + b_ref[...]
        nxt = pltpu.roll(gu, tn - 1, axis=1)
        glu = jnp.minimum(gu, SWIGLU_LIMIT)
        lin = jnp.clip(nxt, -SWIGLU_LIMIT, SWIGLU_LIMIT)
        hdn = glu * _sigmoid(SWIGLU_ALPHA * glu) * (lin + 1.0)
        even = lax.broadcasted_iota(jnp.int32, hdn.shape, 1) % 2 == 0
        hdn = jnp.where(even, hdn, 0.0).astype(BF16)
        sel = sel_ref[...]
        width = sel.shape[0]
        for c in range(tn // width):
            o_ref[:, c * (width // 2):(c + 1) * (width // 2)] = jnp.dot(
                hdn[:, c * width:(c + 1) * width], sel, preferred_element_type=F32).astype(o_ref.dtype)

    @pl.when(valid_ref[i] == 0)
    def _():
        o_ref[...] = jnp.zeros_like(o_ref)


def _expert_up(xs, w_gate_up, b_gate_up, tile_e, tile_valid, tile_first, tm):
    rows_pad, d = xs.shape
    n_exp, _, n2 = w_gate_up.shape
    tn = _pick(n2, 512, MXU_DIM)
    sel = (jnp.arange(MXU_DIM)[:, None] == 2 * jnp.arange(MXU_DIM // 2)[None, :]).astype(BF16)
    grid_spec = pltpu.PrefetchScalarGridSpec(
        num_scalar_prefetch=3,
        grid=(n2 // tn, rows_pad // tm),
        in_specs=[pl.BlockSpec((tm, d), lambda j, i, te, va, fi: (i, 0)),
                  pl.BlockSpec((None, d, tn), lambda j, i, te, va, fi: (te[i], 0, j)),
                  pl.BlockSpec((None, 1, tn), lambda j, i, te, va, fi: (te[i], 0, j)),
                  pl.BlockSpec((MXU_DIM, MXU_DIM // 2), lambda j, i, te, va, fi: (0, 0))],
        out_specs=pl.BlockSpec((tm, tn // 2), lambda j, i, te, va, fi: (i, j)),
        scratch_shapes=[pltpu.VMEM((d, tn), BF16)],
    )
    return pl.pallas_call(
        functools.partial(_expert_up_body, tn=tn),
        grid_spec=grid_spec,
        out_shape=jax.ShapeDtypeStruct((rows_pad, n2 // 2), BF16),
        compiler_params=_params(2, 44),
        name="expert_up",
    )(tile_e, tile_valid, tile_first, xs, w_gate_up, b_gate_up.reshape(n_exp, 1, n2), sel)


def _expert_down_body(te_ref, valid_ref, first_ref, x_ref, w_ref, b_ref, cw_ref, o_ref, wb_ref):
    i = pl.program_id(1)

    @pl.when(first_ref[i] == 1)
    def _():
        wb_ref[...] = w_ref[...].astype(BF16)

    @pl.when(valid_ref[i] == 1)
    def _():
        y = jnp.dot(x_ref[...], wb_ref[...], preferred_element_type=F32) + b_ref[...]
        o_ref[...] = y * cw_ref[...]

    @pl.when(valid_ref[i] == 0)
    def _():
        o_ref[...] = jnp.zeros_like(o_ref)


def _expert_down(hdn, w_down, b_down, row_w, tile_e, tile_valid, tile_first, tm):
    rows_pad, f = hdn.shape
    n_exp, _, d = w_down.shape
    tn = _pick(d, 512, LANES)
    grid_spec = pltpu.PrefetchScalarGridSpec(
        num_scalar_prefetch=3,
        grid=(d // tn, rows_pad // tm),
        in_specs=[pl.BlockSpec((tm, f), lambda j, i, te, va, fi: (i, 0)),
                  pl.BlockSpec((None, f, tn), lambda j, i, te, va, fi: (te[i], 0, j)),
                  pl.BlockSpec((None, 1, tn), lambda j, i, te, va, fi: (te[i], 0, j)),
                  pl.BlockSpec((tm, 1), lambda j, i, te, va, fi: (i, 0))],
        out_specs=pl.BlockSpec((tm, tn), lambda j, i, te, va, fi: (i, j)),
        scratch_shapes=[pltpu.VMEM((f, tn), BF16)],
    )
    return pl.pallas_call(
        _expert_down_body,
        grid_spec=grid_spec,
        out_shape=jax.ShapeDtypeStruct((rows_pad, d), F32),
        compiler_params=_params(2, 44),
        name="expert_down",
    )(tile_e, tile_valid, tile_first, hdn, w_down, b_down.reshape(n_exp, 1, d), row_w)


def _routing_tables(top_idx, top_w, n_exp, tm):
    n_assign = top_idx.size
    rows_pad = -(-(n_assign + n_exp * (tm - 1)) // tm) * tm
    n_tiles = rows_pad // tm
    e_flat = top_idx.reshape(-1)
    onehot = (e_flat[:, None] == jnp.arange(n_exp, dtype=jnp.int32)[None, :]).astype(jnp.int32)
    csum = jnp.cumsum(onehot, axis=0)
    rank = jnp.take_along_axis(csum, e_flat[:, None], axis=1)[:, 0] - 1
    sizes = csum[-1]
    padded = (sizes + tm - 1) // tm * tm
    ends = jnp.cumsum(padded)
    dest = ((ends - padded)[e_flat] + rank).astype(jnp.int32)
    src_tok = jnp.zeros((rows_pad,), jnp.int32).at[dest].set(jnp.arange(n_assign, dtype=jnp.int32) // TOP_K)
    row_w = jnp.zeros((rows_pad,), F32).at[dest].set(top_w.reshape(-1)).reshape(rows_pad, 1)
    tile_start = jnp.arange(n_tiles, dtype=jnp.int32) * tm
    tile_valid = (tile_start < ends[-1]).astype(jnp.int32)
    last_valid = ends[-1] // tm - 1
    tile_of = jnp.minimum(jnp.arange(n_tiles, dtype=jnp.int32), last_valid)
    tile_e = jnp.minimum(jnp.searchsorted(ends, tile_of * tm, side="right"), n_exp - 1).astype(jnp.int32)
    tile_first = jnp.concatenate([jnp.ones((1,), jnp.int32), (tile_e[1:] != tile_e[:-1]).astype(jnp.int32)])
    return dest, src_tok, row_w, tile_e, tile_valid, tile_first, rows_pad


def _layer(x_all, n_p, seq_p, n_s, seq_s, past_len, cache_k, cache_v, state_s, page_table, w, last, final_gain):
    (norm_mix_gain, w_in, sb_bias, ret_gn_gain, w_branch, w_out, norm_ffn_gain, w_router, b_router,
     w_gate_up, b_gate_up, w_down, b_down) = w
    d = x_all.shape[1]
    n_heads_sb, dh = cache_k.shape[2], cache_k.shape[3]
    width = n_heads_sb * dh
    t_p = n_p * seq_p
    t_s = n_s * seq_s
    assert 2 * width == d

    proj = _matmul(_rmsnorm(x_all, norm_mix_gain, BF16), w_in, name="in_proj")

    o_sb = jnp.concatenate([
        _sb_prompt(proj, sb_bias, n_p, seq_p, n_heads_sb, dh),
        _sb_sample(proj, t_p, cache_k, cache_v, page_table, sb_bias, n_s, seq_s, n_heads_sb, dh)], axis=0)
    zero_state = jnp.zeros((n_p,) + state_s.shape[1:], F32)
    o_r_p, state_p = _retention(proj, 0, n_p, seq_p, 0, zero_state, ret_gn_gain, width)
    o_r_s, state_s_new = _retention(proj, t_p, n_s, seq_s, past_len, state_s, ret_gn_gain, width)
    o_r = jnp.concatenate([o_r_p, o_r_s], axis=0)

    merged = _branch_merge(o_sb, o_r, w_branch, proj, d)
    h = _matmul(merged, w_out, residual=x_all, name="out_proj")

    xn, idx_l, wt_l = _router(h, norm_ffn_gain, w_router, b_router)
    tm = MXU_DIM
    dest, src_tok, row_w, tile_e, tile_valid, tile_first, rows_pad = _routing_tables(
        idx_l[:, :TOP_K], wt_l[:, :TOP_K], w_router.shape[1], tm)
    xs = _dispatch(xn, src_tok, rows_pad)
    hdn = _expert_up(xs, w_gate_up, b_gate_up, tile_e, tile_valid, tile_first, tm)
    ys = _expert_down(hdn, w_down, b_down, row_w, tile_e, tile_valid, tile_first, tm)
    y_p = _combine(h, ys, dest, final_gain, 0, t_p, last)
    y_s = _combine(h, ys, dest, final_gain, t_p, t_s, last)

    k_cols = proj[:, width:2 * width]
    v_cols = proj[:, 2 * width:3 * width]
    kv = (k_cols[:t_p].reshape(n_p, seq_p, n_heads_sb, dh), v_cols[:t_p].reshape(n_p, seq_p, n_heads_sb, dh),
          k_cols[t_p:].reshape(n_s, seq_s, n_heads_sb, dh), v_cols[t_p:].reshape(n_s, seq_s, n_heads_sb, dh))
    return y_p, y_s, kv, state_p, state_s_new


def kernel(x_prompt, x_sample, cache_k, cache_v, state_ret, page_table, norm_mix_gain, w_in, sb_bias, ret_gn_gain, w_branch, w_out, norm_ffn_gain, w_router, b_router, w_gate_up, b_gate_up, w_down, b_down, norm_final_gain):
    n_p, seq_p, d = x_prompt.shape
    n_s, seq_s, _ = x_sample.shape
    depth = w_in.shape[0]
    past_len = page_table.shape[1] * cache_k.shape[2]
    y_p = x_prompt.reshape(n_p * seq_p, d)
    y_s = x_sample.reshape(n_s * seq_s, d)
    kp_l, vp_l, sp_l, ks_l, vs_l, ss_l = [], [], [], [], [], []
    for layer in range(depth):
        w = (norm_mix_gain[layer], w_in[layer], sb_bias[layer], ret_gn_gain[layer], w_branch[layer],
             w_out[layer], norm_ffn_gain[layer], w_router[layer], b_router[layer], w_gate_up[layer],
             b_gate_up[layer], w_down[layer], b_down[layer])
        x_all = jnp.concatenate([y_p, y_s], axis=0)
        y_p, y_s, (kp, vp, kn, vn), sp, sn = _layer(
            x_all, n_p, seq_p, n_s, seq_s, past_len, cache_k[layer], cache_v[layer], state_ret[layer],
            page_table, w, layer == depth - 1, norm_final_gain)
        kp_l.append(kp); vp_l.append(vp); sp_l.append(sp)
        ks_l.append(kn); vs_l.append(vn); ss_l.append(sn)
    return (y_p.reshape(n_p, seq_p, d), y_s.reshape(n_s, seq_s, d), jnp.stack(kp_l), jnp.stack(vp_l),
            jnp.stack(sp_l), jnp.stack(ks_l), jnp.stack(vs_l), jnp.stack(ss_l))
```

```python
import functools
import math

import jax
import jax.numpy as jnp
from jax import lax
from jax.experimental import pallas as pl
from jax.experimental.pallas import tpu as pltpu

TOP_K = 4
SWIGLU_LIMIT = 7.0
SWIGLU_ALPHA = 1.702
RMS_EPS = 1e-5
GN_EPS = 1e-6
ROPE_BASE = 10000.0
RET_CHUNK = 128

LANES = 128
MXU_DIM = 256
MIB = 1024 * 1024
HIGHEST = lax.Precision.HIGHEST
F32 = jnp.float32
BF16 = jnp.bfloat16
NEG_BIG = -1e30


def _pick(n, pref, mult=8):
    t = min(pref, n)
    t -= t % mult
    while t >= mult:
        if n % t == 0:
            return t
        t -= mult
    return n


def _params(n_grid, vmem_mib):
    return pltpu.CompilerParams(dimension_semantics=("arbitrary",) * n_grid,
                                vmem_limit_bytes=int(vmem_mib * MIB))


def _sigmoid(x):
    return 1.0 / (1.0 + jnp.exp(-x))


def _softplus(z):
    return jnp.maximum(z, 0.0) + jnp.log(1.0 + jnp.exp(-jnp.abs(z)))


def _dot_nt(a, b, **kw):
    return lax.dot_general(a, b, (((1,), (1,)), ((), ())), preferred_element_type=F32, **kw)


def _dot_tn(a, b, **kw):
    return lax.dot_general(a, b, (((0,), (0,)), ((), ())), preferred_element_type=F32, **kw)


def _rmsnorm_body(x_ref, g_ref, o_ref):
    x = x_ref[...]
    ms = jnp.mean(x * x, axis=-1, keepdims=True)
    o_ref[...] = (x * lax.rsqrt(ms + RMS_EPS) * g_ref[...]).astype(o_ref.dtype)


def _rmsnorm(x, gain, out_dtype):
    t, d = x.shape
    tm = _pick(t, 256)
    return pl.pallas_call(
        _rmsnorm_body,
        grid=(t // tm,),
        in_specs=[pl.BlockSpec((tm, d), lambda i: (i, 0)),
                  pl.BlockSpec((1, d), lambda i: (0, 0))],
        out_specs=pl.BlockSpec((tm, d), lambda i: (i, 0)),
        out_shape=jax.ShapeDtypeStruct((t, d), out_dtype),
        compiler_params=_params(1, 32),
        name="rmsnorm",
    )(x, gain.reshape(1, d))


def _mm_body(a_ref, w_ref, o_ref, wb_ref):
    @pl.when(pl.program_id(1) == 0)
    def _():
        wb_ref[...] = w_ref[...].astype(BF16)

    o_ref[...] = jnp.dot(a_ref[...], wb_ref[...], preferred_element_type=F32)


def _mm_res_body(a_ref, w_ref, r_ref, o_ref, wb_ref):
    @pl.when(pl.program_id(1) == 0)
    def _():
        wb_ref[...] = w_ref[...].astype(BF16)

    o_ref[...] = r_ref[...] + jnp.dot(a_ref[...], wb_ref[...], preferred_element_type=F32)


def _matmul(a, w, residual=None, name="matmul"):
    m, k = a.shape
    n = w.shape[1]
    tm, tn = _pick(m, 512), _pick(n, 512, LANES)
    in_specs = [pl.BlockSpec((tm, k), lambda j, i: (i, 0)),
                pl.BlockSpec((k, tn), lambda j, i: (0, j))]
    args = [a, w]
    body = _mm_body
    if residual is not None:
        in_specs.append(pl.BlockSpec((tm, tn), lambda j, i: (i, j)))
        args.append(residual)
        body = _mm_res_body
    return pl.pallas_call(
        body,
        grid=(n // tn, m // tm),
        in_specs=in_specs,
        out_specs=pl.BlockSpec((tm, tn), lambda j, i: (i, j)),
        out_shape=jax.ShapeDtypeStruct((m, n), F32),
        scratch_shapes=[pltpu.VMEM((k, tn), BF16)],
        compiler_params=_params(2, 44),
        name=name,
    )(*args)


def _branch_body(a0_ref, a1_ref, w0_ref, w1_ref, ga_ref, gb_ref, o_ref, w0b_ref, w1b_ref):
    @pl.when(pl.program_id(1) == 0)
    def _():
        w0b_ref[...] = w0_ref[...].astype(BF16)
        w1b_ref[...] = w1_ref[...].astype(BF16)

    p0 = jnp.dot(a0_ref[...], w0b_ref[...], preferred_element_type=F32)
    p1 = jnp.dot(a1_ref[...], w1b_ref[...], preferred_element_type=F32)
    o_ref[...] = (_sigmoid(ga_ref[...]) * p0 + _sigmoid(gb_ref[...]) * p1).astype(o_ref.dtype)


def _branch_merge(o_sb, o_r, w_branch, proj, d_model):
    t, w = o_sb.shape
    tm, tn = _pick(t, 512), _pick(d_model, 512, LANES)
    ga_blk = 3 * d_model // tn
    gb_blk = 4 * d_model // tn
    return pl.pallas_call(
        _branch_body,
        grid=(d_model // tn, t // tm),
        in_specs=[pl.BlockSpec((tm, w), lambda j, i: (i, 0)),
                  pl.BlockSpec((tm, w), lambda j, i: (i, 0)),
                  pl.BlockSpec((None, w, tn), lambda j, i: (0, 0, j)),
                  pl.BlockSpec((None, w, tn), lambda j, i: (1, 0, j)),
                  pl.BlockSpec((tm, tn), lambda j, i: (i, ga_blk + j)),
                  pl.BlockSpec((tm, tn), lambda j, i: (i, gb_blk + j))],
        out_specs=pl.BlockSpec((tm, tn), lambda j, i: (i, j)),
        out_shape=jax.ShapeDtypeStruct((t, d_model), BF16),
        scratch_shapes=[pltpu.VMEM((w, tn), BF16), pltpu.VMEM((w, tn), BF16)],
        compiler_params=_params(2, 44),
        name="branch_merge",
    )(o_sb, o_r, w_branch, w_branch, proj, proj)


def _reverse_cumsum(sp, tri):
    hi = sp.astype(BF16)
    lo = (sp - hi.astype(F32)).astype(BF16)
    return (jnp.dot(hi, tri, preferred_element_type=F32) + jnp.dot(lo, tri, preferred_element_type=F32))


def _sb_prompt_body(qi_ref, kb_ref, bias_ref, q_ref, k_ref, v_ref, tri_ref, o_ref, acc_ref, carry_ref,
                    *, blk, sub, scale):
    h = pl.program_id(1)
    p = pl.program_id(2)
    qi = qi_ref[p]
    kb = kb_ref[p]

    def step(diagonal):
        q = q_ref[...].astype(BF16)
        k = k_ref[...].astype(BF16)
        z = _dot_nt(q, k) * scale + bias_ref[h]
        sp = _softplus(z)
        if diagonal:
            visible = (lax.broadcasted_iota(jnp.int32, (blk, blk), 1)
                       < lax.broadcasted_iota(jnp.int32, (blk, blk), 0))
            sp = jnp.where(visible, sp, 0.0)
        tri = tri_ref[...]
        carry = carry_ref[...]
        weights = [None] * (blk // sub)
        for c in reversed(range(blk // sub)):
            cols = slice(c * sub, (c + 1) * sub)
            cs = _reverse_cumsum(sp[:, cols], tri)
            arg = z[:, cols] - cs - carry
            if diagonal:
                arg = jnp.where(visible[:, cols], arg, NEG_BIG)
            weights[c] = jnp.exp(arg).astype(BF16)
            carry = carry + cs[:, 0:1]
        carry_ref[...] = carry
        w = jnp.concatenate(weights, axis=1)
        acc_ref[...] += jnp.dot(w, v_ref[...].astype(BF16), preferred_element_type=F32)

    @pl.when(kb == qi)
    def _():
        acc_ref[...] = jnp.zeros_like(acc_ref)
        carry_ref[...] = jnp.zeros_like(carry_ref)
        step(True)

    @pl.when(kb != qi)
    def _():
        step(False)

    @pl.when(kb == 0)
    def _():
        o_ref[...] = acc_ref[...].astype(o_ref.dtype)


def _sb_prompt(proj, sb_bias, n_seq, seq, n_heads, dh):
    blk = _pick(seq, 512, LANES)
    sub = _pick(blk, MXU_DIM, LANES)
    nq = seq // blk
    pairs = [(qi, kb) for qi in range(nq) for kb in range(qi, -1, -1)]
    qi_tab = jnp.array([p[0] for p in pairs], jnp.int32)
    kb_tab = jnp.array([p[1] for p in pairs], jnp.int32)
    tri = jnp.tril(jnp.ones((sub, sub), F32)).astype(BF16)
    grid_spec = pltpu.PrefetchScalarGridSpec(
        num_scalar_prefetch=2,
        grid=(n_seq, n_heads, len(pairs)),
        in_specs=[pl.BlockSpec(memory_space=pltpu.SMEM),
                  pl.BlockSpec((blk, dh), lambda b, h, p, qi, kb: (b * nq + qi[p], h)),
                  pl.BlockSpec((blk, dh), lambda b, h, p, qi, kb: (b * nq + kb[p], n_heads + h)),
                  pl.BlockSpec((blk, dh), lambda b, h, p, qi, kb: (b * nq + kb[p], 2 * n_heads + h)),
                  pl.BlockSpec((sub, sub), lambda b, h, p, qi, kb: (0, 0))],
        out_specs=pl.BlockSpec((blk, dh), lambda b, h, p, qi, kb: (b * nq + qi[p], h)),
        scratch_shapes=[pltpu.VMEM((blk, dh), F32), pltpu.VMEM((blk, 1), F32)],
    )
    return pl.pallas_call(
        functools.partial(_sb_prompt_body, blk=blk, sub=sub, scale=dh ** -0.5),
        grid_spec=grid_spec,
        out_shape=jax.ShapeDtypeStruct((n_seq * seq, n_heads * dh), BF16),
        compiler_params=_params(3, 40),
        name="sb_prompt",
    )(qi_tab, kb_tab, sb_bias.astype(F32), proj, proj, proj, tri)


def _sb_sample_body(pt_ref, q_ref, kn_ref, vn_ref, *rest, n_heads, dh, n_new, page, scale, ppg):
    kp_refs, vp_refs = rest[:ppg], rest[ppg:2 * ppg]
    bias_ref, tri_ref, o_ref, acc_ref, carry_ref = rest[2 * ppg:]
    p = pl.program_id(1)
    rows = n_heads * n_new
    bias = bias_ref[...]
    tri = tri_ref[...]

    def attend(get_k, get_v, masked, acc, carry):
        z = jnp.concatenate(
            [_dot_nt(q_ref[:, h * dh:(h + 1) * dh].astype(BF16), get_k(h)) for h in range(n_heads)], axis=0)
        z = z * scale + bias
        sp = _softplus(z)
        if masked:
            t = lax.broadcasted_iota(jnp.int32, (rows, page), 0) % n_new
            s = lax.broadcasted_iota(jnp.int32, (rows, page), 1)
            visible = s < t
            sp = jnp.where(visible, sp, 0.0)
        cs = _reverse_cumsum(sp, tri)
        arg = z - cs - carry
        if masked:
            arg = jnp.where(visible, arg, NEG_BIG)
        w = jnp.exp(arg).astype(BF16)
        out = jnp.concatenate(
            [jnp.dot(w[h * n_new:(h + 1) * n_new, :], get_v(h), preferred_element_type=F32)
             for h in range(n_heads)], axis=0)
        return acc + out, carry + cs[:, 0:1]

    @pl.when(p == 0)
    def _():
        pad = jnp.zeros((page - n_new, dh), BF16)

        def new_rows(ref, h):
            return jnp.concatenate([ref[:, h * dh:(h + 1) * dh].astype(BF16), pad], axis=0)

        acc, carry = attend(functools.partial(new_rows, kn_ref), functools.partial(new_rows, vn_ref), True,
                            jnp.zeros(acc_ref.shape, F32), jnp.zeros(carry_ref.shape, F32))
        acc_ref[...] = acc
        carry_ref[...] = carry

    @pl.when(p > 0)
    def _():
        def past_rows(ref, h):
            return ref[pl.ds(h, page, stride=n_heads), :].astype(BF16)

        acc, carry = acc_ref[...], carry_ref[...]
        for u in range(ppg):
            acc, carry = attend(functools.partial(past_rows, kp_refs[u]), functools.partial(past_rows, vp_refs[u]),
                                False, acc, carry)
        acc_ref[...] = acc
        carry_ref[...] = carry

    @pl.when(p == pl.num_programs(1) - 1)
    def _():
        for h in range(n_heads):
            o_ref[:, h * dh:(h + 1) * dh] = acc_ref[h * n_new:(h + 1) * n_new, :].astype(o_ref.dtype)


def _sb_sample(proj, row0, cache_k, cache_v, page_table, sb_bias, n_seq, n_new, n_heads, dh):
    n_pool, page = cache_k.shape[0], cache_k.shape[1]
    n_pages = page_table.shape[1]
    width = n_heads * dh
    ppg = _pick(n_pages, 4, 1)
    assert row0 % n_new == 0 and n_new % 8 == 0 and page == LANES
    blk0 = row0 // n_new
    kc = cache_k.reshape(n_pool, page * n_heads, dh)
    vc = cache_v.reshape(n_pool, page * n_heads, dh)
    bias_rows = jnp.repeat(sb_bias.astype(F32), n_new).reshape(n_heads * n_new, 1)
    tri = jnp.tril(jnp.ones((page, page), F32)).astype(BF16)

    def page_idx(u, b, p, pt):
        return (pt[b * n_pages + n_pages - (jnp.maximum(p, 1) - 1) * ppg - 1 - u], 0, 0)

    page_specs = [pl.BlockSpec((None, page * n_heads, dh), functools.partial(page_idx, u)) for u in range(ppg)]
    grid_spec = pltpu.PrefetchScalarGridSpec(
        num_scalar_prefetch=1,
        grid=(n_seq, n_pages // ppg + 1),
        in_specs=[pl.BlockSpec((n_new, width), lambda b, p, pt: (blk0 + b, 0)),
                  pl.BlockSpec((n_new, width), lambda b, p, pt: (blk0 + b, 1)),
                  pl.BlockSpec((n_new, width), lambda b, p, pt: (blk0 + b, 2))]
        + page_specs + page_specs
        + [pl.BlockSpec((n_heads * n_new, 1), lambda b, p, pt: (0, 0)),
           pl.BlockSpec((page, page), lambda b, p, pt: (0, 0))],
        out_specs=pl.BlockSpec((n_new, width), lambda b, p, pt: (b, 0)),
        scratch_shapes=[pltpu.VMEM((n_heads * n_new, dh), F32), pltpu.VMEM((n_heads * n_new, 1), F32)],
    )
    return pl.pallas_call(
        functools.partial(_sb_sample_body, n_heads=n_heads, dh=dh, n_new=n_new, page=page, scale=dh ** -0.5,
                          ppg=ppg),
        grid_spec=grid_spec,
        out_shape=jax.ShapeDtypeStruct((n_seq * n_new, width), BF16),
        compiler_params=_params(2, 40),
        name="sb_sample",
    )(page_table.reshape(-1).astype(jnp.int32), proj, proj, proj, *([kc] * ppg), *([vc] * ppg), bias_rows, tri)


def _retention_body(q_ref, k_ref, v_ref, g_ref, cos_ref, sin_ref, gain_ref, s0_ref, o_ref, sout_ref,
                    state_ref, *, n_heads, dk, dv, chunk):
    c = pl.program_id(1)

    @pl.when(c == 0)
    def _():
        state_ref[...] = s0_ref[...]

    cosf = cos_ref[...]
    sinf = sin_ref[...]
    ri = lax.broadcasted_iota(jnp.int32, (chunk, chunk), 0)
    ci = lax.broadcasted_iota(jnp.int32, (chunk, chunk), 1)
    diff = (ri - ci).astype(F32)
    idx = lax.broadcasted_iota(jnp.int32, (chunk, 1), 0).astype(F32)
    for h in range(n_heads):
        log_gamma = math.log(1.0 - 2.0 ** (-5.0 - h))
        q = q_ref[:, h * dk:(h + 1) * dk]
        k = k_ref[:, h * dk:(h + 1) * dk]
        v = v_ref[:, h * dv:(h + 1) * dv]
        q = q * cosf + pltpu.roll(q, dk // 2, axis=1) * sinf
        k = (k * cosf + pltpu.roll(k, dk // 2, axis=1) * sinf) * (dk ** -0.5)
        state = state_ref[h]
        intra = jnp.where(diff >= 0, jnp.exp(log_gamma * jnp.maximum(diff, 0.0)), 0.0)
        scores = _dot_nt(q, k, precision=HIGHEST) * intra
        inner = jnp.dot(scores, v, precision=HIGHEST, preferred_element_type=F32)
        cross = jnp.dot(q, state, precision=HIGHEST, preferred_element_type=F32) * jnp.exp(log_gamma * (idx + 1.0))
        k_dec = k * jnp.exp(log_gamma * (chunk - 1.0 - idx))
        state_ref[h] = state * math.exp(log_gamma * chunk) + _dot_tn(k_dec, v, precision=HIGHEST)
        o = inner + cross
        mu = jnp.mean(o, axis=-1, keepdims=True)
        var = jnp.mean(jnp.square(o - mu), axis=-1, keepdims=True)
        o = (o - mu) * lax.rsqrt(var + GN_EPS) * gain_ref[h:h + 1, :]
        g = g_ref[:, h * dv:(h + 1) * dv]
        o_ref[:, h * dv:(h + 1) * dv] = (g * _sigmoid(g) * o).astype(o_ref.dtype)

    @pl.when(c == pl.num_programs(1) - 1)
    def _():
        sout_ref[...] = state_ref[...]


def _retention(proj, row0, n_seq, seq, pos0, state0, gn_gain, width):
    n_heads, dk, dv = state0.shape[1], state0.shape[2], state0.shape[3]
    chunk = RET_CHUNK if seq % RET_CHUNK == 0 else seq
    n_chunks = seq // chunk
    qk_w = n_heads * dk
    assert n_heads * dv == width and 2 * qk_w == width and row0 % chunk == 0
    blk0 = row0 // chunk
    half = dk // 2
    inv_freq = ROPE_BASE ** (-jnp.arange(half, dtype=F32) / half)
    ang = (pos0 + jnp.arange(seq, dtype=jnp.int32)).astype(F32)[:, None] * inv_freq[None, :]
    cos_t = jnp.concatenate([jnp.cos(ang), jnp.cos(ang)], axis=1)
    sin_t = jnp.concatenate([-jnp.sin(ang), jnp.sin(ang)], axis=1)
    row = lambda b, c: blk0 + b * n_chunks + c
    return pl.pallas_call(
        functools.partial(_retention_body, n_heads=n_heads, dk=dk, dv=dv, chunk=chunk),
        grid=(n_seq, n_chunks),
        in_specs=[pl.BlockSpec((chunk, qk_w), lambda b, c: (row(b, c), 6)),
                  pl.BlockSpec((chunk, qk_w), lambda b, c: (row(b, c), 7)),
                  pl.BlockSpec((chunk, width), lambda b, c: (row(b, c), 4)),
                  pl.BlockSpec((chunk, width), lambda b, c: (row(b, c), 5)),
                  pl.BlockSpec((chunk, dk), lambda b, c: (c, 0)),
                  pl.BlockSpec((chunk, dk), lambda b, c: (c, 0)),
                  pl.BlockSpec((n_heads, dv), lambda b, c: (0, 0)),
                  pl.BlockSpec((None, n_heads, dk, dv), lambda b, c: (b, 0, 0, 0))],
        out_specs=[pl.BlockSpec((chunk, width), lambda b, c: (b * n_chunks + c, 0)),
                   pl.BlockSpec((None, n_heads, dk, dv), lambda b, c: (b, 0, 0, 0))],
        out_shape=[jax.ShapeDtypeStruct((n_seq * seq, width), BF16),
                   jax.ShapeDtypeStruct(state0.shape, F32)],
        scratch_shapes=[pltpu.VMEM((n_heads, dk, dv), F32)],
        compiler_params=_params(2, 32),
        name="retention",
    )(proj, proj, proj, proj, cos_t, sin_t, gn_gain.astype(F32), state0.astype(F32))


def _router_body(h_ref, g_ref, wr_ref, br_ref, xn_ref, idx_ref, wt_ref):
    x = h_ref[...]
    ms = jnp.mean(x * x, axis=-1, keepdims=True)
    xn = x * lax.rsqrt(ms + RMS_EPS) * g_ref[...]
    xn_ref[...] = xn
    logits = jnp.dot(xn, wr_ref[...], precision=HIGHEST, preferred_element_type=F32) + br_ref[...]
    lane = lax.broadcasted_iota(jnp.int32, logits.shape, 1)
    vals, idxs = [], []
    work = logits
    for _ in range(TOP_K):
        m = jnp.max(work, axis=-1, keepdims=True)
        i = jnp.min(jnp.where(work == m, lane, LANES), axis=-1, keepdims=True)
        vals.append(m)
        idxs.append(i)
        work = jnp.where(lane == i, -jnp.inf, work)
    exps = [jnp.exp(v - vals[0]) for v in vals]
    denom = exps[0] + exps[1] + exps[2] + exps[3]
    idx_out = jnp.zeros(logits.shape, jnp.int32)
    wt_out = jnp.zeros(logits.shape, F32)
    for k in range(TOP_K):
        idx_out = jnp.where(lane == k, idxs[k], idx_out)
        wt_out = jnp.where(lane == k, exps[k] / denom, wt_out)
    idx_ref[...] = idx_out
    wt_ref[...] = wt_out


def _router(h, gain, w_router, b_router):
    t, d = h.shape
    e = w_router.shape[1]
    assert e <= LANES
    tm = _pick(t, 256)
    wr = jnp.zeros((d, LANES), F32).at[:, :e].set(w_router)
    br = jnp.full((1, LANES), NEG_BIG, F32).at[0, :e].set(b_router)
    return pl.pallas_call(
        _router_body,
        grid=(t // tm,),
        in_specs=[pl.BlockSpec((tm, d), lambda i: (i, 0)),
                  pl.BlockSpec((1, d), lambda i: (0, 0)),
                  pl.BlockSpec((d, LANES), lambda i: (0, 0)),
                  pl.BlockSpec((1, LANES), lambda i: (0, 0))],
        out_specs=[pl.BlockSpec((tm, d), lambda i: (i, 0)),
                   pl.BlockSpec((tm, LANES), lambda i: (i, 0)),
                   pl.BlockSpec((tm, LANES), lambda i: (i, 0))],
        out_shape=[jax.ShapeDtypeStruct((t, d), F32),
                   jax.ShapeDtypeStruct((t, LANES), jnp.int32),
                   jax.ShapeDtypeStruct((t, LANES), F32)],
        compiler_params=_params(1, 40),
        name="router",
    )(h, gain.reshape(1, d), wr, br)


def _row_copy(src_hbm, src_row, dst, dst_row, sem):
    return pltpu.make_async_copy(src_hbm.at[pl.ds(src_row, 1)], dst.at[pl.ds(dst_row, 1)], sem)


def _dispatch_body(src_ref, used_ref, x_hbm, o_ref, buf_ref, sem, *, rows):
    i = pl.program_id(0)
    slot = i % 2

    def issue(tile, to_slot):
        def body(r, carry):
            _row_copy(x_hbm, src_ref[tile * rows + r], buf_ref.at[to_slot], r, sem.at[to_slot]).start()
            return carry

        lax.fori_loop(0, rows, body, 0, unroll=8)

    last = pl.num_programs(0) - 1

    @pl.when(jnp.logical_and(i == 0, used_ref[0] == 1))
    def _():
        issue(0, 0)

    @pl.when(jnp.logical_and(i < last, used_ref[jnp.minimum(i + 1, last)] == 1))
    def _():
        issue(i + 1, 1 - slot)

    @pl.when(used_ref[i] == 1)
    def _():
        def drain(r, carry):
            _row_copy(x_hbm, 0, buf_ref.at[slot], r, sem.at[slot]).wait()
            return carry

        lax.fori_loop(0, rows, drain, 0, unroll=8)
        o_ref[...] = buf_ref[slot].astype(o_ref.dtype)

    @pl.when(used_ref[i] == 0)
    def _():
        o_ref[...] = jnp.zeros_like(o_ref)


def _dispatch(xn, src_tok, tile_used, rows_pad, rows):
    d = xn.shape[1]
    grid_spec = pltpu.PrefetchScalarGridSpec(
        num_scalar_prefetch=2,
        grid=(rows_pad // rows,),
        in_specs=[pl.BlockSpec(memory_space=pl.ANY)],
        out_specs=pl.BlockSpec((rows, d), lambda i, s, u: (i, 0)),
        scratch_shapes=[pltpu.VMEM((2, rows, d), F32), pltpu.SemaphoreType.DMA((2,))],
    )
    return pl.pallas_call(
        functools.partial(_dispatch_body, rows=rows),
        grid_spec=grid_spec,
        out_shape=jax.ShapeDtypeStruct((rows_pad, d), BF16),
        compiler_params=_params(1, 32),
        name="moe_dispatch",
    )(src_tok, tile_used, xn)


def _combine_body(pos_ref, h_ref, wt_ref, ys_hbm, g_ref, o_ref, buf_ref, sem, *, rows, row0, final_norm):
    base = (row0 + pl.program_id(0) * rows) * TOP_K

    def issue(r, carry):
        for k in range(TOP_K):
            _row_copy(ys_hbm, pos_ref[base + r * TOP_K + k], buf_ref.at[k], r, sem).start()
        return carry

    lax.fori_loop(0, rows, issue, 0)

    def drain(r, carry):
        for k in range(TOP_K):
            _row_copy(ys_hbm, 0, buf_ref.at[k], r, sem).wait()
        return carry

    lax.fori_loop(0, rows, drain, 0)
    wt = wt_ref[...]
    moe = ((wt[:, 0:1] * buf_ref[0] + wt[:, 1:2] * buf_ref[1])
           + (wt[:, 2:3] * buf_ref[2] + wt[:, 3:4] * buf_ref[3]))
    y = h_ref[...] + moe
    if final_norm:
        ms = jnp.mean(y * y, axis=-1, keepdims=True)
        y = y * lax.rsqrt(ms + RMS_EPS) * g_ref[...]
    o_ref[...] = y


def _combine(h, wt, ys, pos, gain, row0, n_rows, final_norm):
    d = h.shape[1]
    rows = _pick(n_rows, 64)
    assert row0 % rows == 0
    blk0 = row0 // rows
    grid_spec = pltpu.PrefetchScalarGridSpec(
        num_scalar_prefetch=1,
        grid=(n_rows // rows,),
        in_specs=[pl.BlockSpec((rows, d), lambda i, s: (blk0 + i, 0)),
                  pl.BlockSpec((rows, LANES), lambda i, s: (blk0 + i, 0)),
                  pl.BlockSpec(memory_space=pl.ANY),
                  pl.BlockSpec((1, d), lambda i, s: (0, 0))],
        out_specs=pl.BlockSpec((rows, d), lambda i, s: (i, 0)),
        scratch_shapes=[pltpu.VMEM((TOP_K, rows, d), F32), pltpu.SemaphoreType.DMA(())],
    )
    return pl.pallas_call(
        functools.partial(_combine_body, rows=rows, row0=row0, final_norm=final_norm),
        grid_spec=grid_spec,
        out_shape=jax.ShapeDtypeStruct((n_rows, d), F32),
        compiler_params=_params(1, 32),
        name="moe_combine",
    )(pos, h, wt, ys, gain.reshape(1, d))


def _swiglu_compact(gu, sel, tn):
    nxt = pltpu.roll(gu, tn - 1, axis=1)
    glu = jnp.minimum(gu, SWIGLU_LIMIT)
    lin = jnp.clip(nxt, -SWIGLU_LIMIT, SWIGLU_LIMIT)
    hdn = glu * _sigmoid(SWIGLU_ALPHA * glu) * (lin + 1.0)
    even = lax.broadcasted_iota(jnp.int32, hdn.shape, 1) % 2 == 0
    hdn = jnp.where(even, hdn, 0.0).astype(BF16)
    width = sel.shape[0]
    return jnp.concatenate(
        [jnp.dot(hdn[:, c * width:(c + 1) * width], sel, preferred_element_type=F32).astype(BF16)
         for c in range(tn // width)], axis=1)


def _expert_up_body(be_ref, used_ref, src_ref, x_ref, w_ref, b_ref, sel_ref, o_ref, wb_ref, *, tm, tn):
    @pl.when(used_ref[pl.program_id(0)] == 1)
    def _():
        wb_ref[...] = w_ref[...].astype(BF16)
        for g in range(x_ref.shape[0] // tm):
            rows = slice(g * tm, (g + 1) * tm)
            gu = jnp.dot(x_ref[rows, :], wb_ref[...], preferred_element_type=F32) + b_ref[...]
            o_ref[rows, :] = _swiglu_compact(gu, sel_ref[...], tn)

    @pl.when(used_ref[pl.program_id(0)] == 0)
    def _():
        o_ref[...] = jnp.zeros_like(o_ref)


def _expert_down_body(be_ref, used_ref, src_ref, x_ref, w_ref, b_ref, o_ref, wb_ref, *, tm):
    @pl.when(used_ref[pl.program_id(0)] == 1)
    def _():
        wb_ref[...] = w_ref[...].astype(BF16)
        for g in range(x_ref.shape[0] // tm):
            rows = slice(g * tm, (g + 1) * tm)
            o_ref[rows, :] = jnp.dot(x_ref[rows, :], wb_ref[...], preferred_element_type=F32) + b_ref[...]

    @pl.when(used_ref[pl.program_id(0)] == 0)
    def _():
        o_ref[...] = jnp.zeros_like(o_ref)


def _expert_matmul(body, name, xs, w, b, extra, out_cols, out_dtype, tabs, rb, tn):
    block_e, block_used, block_src = tabs
    n_exp, k, n = w.shape
    n_j = n // tn
    tn_out = tn * out_cols // n

    def x_idx(s, j, be, used, src):
        return (src[s], 0)

    def w_idx(s, j, be, used, src):
        return (be[s], 0, jnp.where(used[s] == 1, j, n_j - 1))

    def o_idx(s, j, be, used, src):
        return (s, j)

    grid_spec = pltpu.PrefetchScalarGridSpec(
        num_scalar_prefetch=3,
        grid=(block_e.shape[0], n_j),
        in_specs=[pl.BlockSpec((rb, k), x_idx),
                  pl.BlockSpec((None, k, tn), w_idx),
                  pl.BlockSpec((None, 1, tn), w_idx)]
        + [pl.BlockSpec(e.shape, lambda s, j, be, used, src: (0, 0)) for e in extra],
        out_specs=pl.BlockSpec((rb, tn_out), o_idx),
        scratch_shapes=[pltpu.VMEM((k, tn), BF16)],
    )
    return pl.pallas_call(
        body,
        grid_spec=grid_spec,
        out_shape=jax.ShapeDtypeStruct((xs.shape[0], out_cols), out_dtype),
        compiler_params=_params(2, 56),
        name=name,
    )(block_e, block_used, block_src, xs, w, b.reshape(n_exp, 1, n), *extra)


def _expert_up(xs, w_gate_up, b_gate_up, tabs, rb, tm):
    n2 = w_gate_up.shape[2]
    tn = _pick(n2, 512, MXU_DIM)
    sel = (jnp.arange(MXU_DIM)[:, None] == 2 * jnp.arange(MXU_DIM // 2)[None, :]).astype(BF16)
    return _expert_matmul(functools.partial(_expert_up_body, tm=tm, tn=tn), "expert_up", xs, w_gate_up, b_gate_up,
                          [sel], n2 // 2, BF16, tabs, rb, tn)


def _expert_down(hdn, w_down, b_down, tabs, rb, tm):
    d = w_down.shape[2]
    return _expert_matmul(functools.partial(_expert_down_body, tm=tm), "expert_down", hdn, w_down, b_down,
                          [], d, F32, tabs, rb, _pick(d, 512, LANES))


def _lookup(table, idx):
    hit = idx[:, None] == jnp.arange(table.shape[0], dtype=jnp.int32)[None, :]
    return jnp.sum(jnp.where(hit, table[None, :], 0), axis=1)


def _routing_tables(top_idx, n_exp, tm, rb):
    n_assign = top_idx.size
    n_blocks = -(-(n_assign + n_exp * (rb - 1)) // rb)
    e_flat = top_idx.reshape(-1)
    lanes = jnp.arange(n_exp, dtype=jnp.int32)
    onehot = e_flat[:, None] == lanes[None, :]
    chunk = _pick(n_assign, 512)
    oh = onehot.astype(BF16).reshape(n_assign // chunk, chunk, n_exp)
    tri = jnp.tril(jnp.ones((chunk, chunk), F32)).astype(BF16)
    within = jnp.einsum("ij,bje->bie", tri, oh, preferred_element_type=F32)
    totals = within[:, -1, :].astype(jnp.int32)
    before = jnp.cumsum(totals, axis=0) - totals
    csum = within.astype(jnp.int32) + before[:, None, :]
    sizes = before[-1] + totals[-1]
    padded = (sizes + rb - 1) // rb * rb
    ends = jnp.cumsum(padded)
    starts = ends - padded
    dest = jnp.sum(jnp.where(onehot, csum.reshape(n_assign, n_exp) - 1 + starts[None, :], 0), axis=1)
    dest = dest.astype(jnp.int32)
    rows_pad = n_blocks * rb
    src_tok = jnp.zeros((rows_pad,), jnp.int32).at[dest].set(jnp.arange(n_assign, dtype=jnp.int32) // TOP_K)
    block_ids = jnp.arange(n_blocks, dtype=jnp.int32)
    n_used = ends[-1] // rb
    block_used = (block_ids < n_used).astype(jnp.int32)
    block_src = jnp.minimum(block_ids, n_used - 1)
    block_e = jnp.minimum(jnp.sum((ends[None, :] <= (block_src * rb)[:, None]).astype(jnp.int32), axis=1), n_exp - 1)
    n_tiles = rows_pad // tm
    tile_start = jnp.arange(n_tiles, dtype=jnp.int32) * tm
    tile_blk = jnp.minimum(tile_start // rb, n_blocks - 1)
    tile_e = _lookup(block_e, tile_blk)
    real_end = _lookup(starts + sizes, tile_e)
    tile_used = ((tile_start < real_end) & (_lookup(block_used, tile_blk) == 1)
                 & (tile_start < n_blocks * rb)).astype(jnp.int32)
    return dest, src_tok, (block_e.astype(jnp.int32), block_used, block_src), tile_used, rows_pad


def _layer(x_all, n_p, seq_p, n_s, seq_s, past_len, cache_k, cache_v, state_s, page_table, w, last, final_gain):
    (norm_mix_gain, w_in, sb_bias, ret_gn_gain, w_branch, w_out, norm_ffn_gain, w_router, b_router,
     w_gate_up, b_gate_up, w_down, b_down) = w
    d = x_all.shape[1]
    n_heads_sb, dh = cache_k.shape[2], cache_k.shape[3]
    width = n_heads_sb * dh
    t_p = n_p * seq_p
    t_s = n_s * seq_s
    assert 2 * width == d

    proj = _matmul(_rmsnorm(x_all, norm_mix_gain, BF16), w_in, name="in_proj")

    o_sb = jnp.concatenate([
        _sb_prompt(proj, sb_bias, n_p, seq_p, n_heads_sb, dh),
        _sb_sample(proj, t_p, cache_k, cache_v, page_table, sb_bias, n_s, seq_s, n_heads_sb, dh)], axis=0)
    zero_state = jnp.zeros((n_p,) + state_s.shape[1:], F32)
    o_r_p, state_p = _retention(proj, 0, n_p, seq_p, 0, zero_state, ret_gn_gain, width)
    o_r_s, state_s_new = _retention(proj, t_p, n_s, seq_s, past_len, state_s, ret_gn_gain, width)
    o_r = jnp.concatenate([o_r_p, o_r_s], axis=0)

    merged = _branch_merge(o_sb, o_r, w_branch, proj, d)
    h = _matmul(merged, w_out, residual=x_all, name="out_proj")

    xn, idx_l, wt_l = _router(h, norm_ffn_gain, w_router, b_router)
    n_exp = w_router.shape[1]
    tm = MXU_DIM
    rb = tm * -(-11 * TOP_K * h.shape[0] // (10 * n_exp * tm))
    dest, src_tok, tabs, tile_used, rows_pad = _routing_tables(idx_l[:, :TOP_K], n_exp, tm, rb)
    xs = _dispatch(xn, src_tok, tile_used, rows_pad, tm)
    hdn = _expert_up(xs, w_gate_up, b_gate_up, tabs, rb, tm)
    ys = _expert_down(hdn, w_down, b_down, tabs, rb, tm)
    y_p = _combine(h, wt_l, ys, dest, final_gain, 0, t_p, last)
    y_s = _combine(h, wt_l, ys, dest, final_gain, t_p, t_s, last)

    k_cols = proj[:, width:2 * width]
    v_cols = proj[:, 2 * width:3 * width]
    kv = (k_cols[:t_p].reshape(n_p, seq_p, n_heads_sb, dh), v_cols[:t_p].reshape(n_p, seq_p, n_heads_sb, dh),
          k_cols[t_p:].reshape(n_s, seq_s, n_heads_sb, dh), v_cols[t_p:].reshape(n_s, seq_s, n_heads_sb, dh))
    return y_p, y_s, kv, state_p, state_s_new


def kernel(x_prompt, x_sample, cache_k, cache_v, state_ret, page_table, norm_mix_gain, w_in, sb_bias, ret_gn_gain, w_branch, w_out, norm_ffn_gain, w_router, b_router, w_gate_up, b_gate_up, w_down, b_down, norm_final_gain):
    n_p, seq_p, d = x_prompt.shape
    n_s, seq_s, _ = x_sample.shape
    depth = w_in.shape[0]
    past_len = page_table.shape[1] * cache_k.shape[2]
    y_p = x_prompt.reshape(n_p * seq_p, d)
    y_s = x_sample.reshape(n_s * seq_s, d)
    kp_l, vp_l, sp_l, ks_l, vs_l, ss_l = [], [], [], [], [], []
    for layer in range(depth):
        w = (norm_mix_gain[layer], w_in[layer], sb_bias[layer], ret_gn_gain[layer], w_branch[layer],
             w_out[layer], norm_ffn_gain[layer], w_router[layer], b_router[layer], w_gate_up[layer],
             b_gate_up[layer], w_down[layer], b_down[layer])
        x_all = jnp.concatenate([y_p, y_s], axis=0)
        y_p, y_s, (kp, vp, kn, vn), sp, sn = _layer(
            x_all, n_p, seq_p, n_s, seq_s, past_len, cache_k[layer], cache_v[layer], state_ret[layer],
            page_table, w, layer == depth - 1, norm_final_gain)
        kp_l.append(kp); vp_l.append(vp); sp_l.append(sp)
        ks_l.append(kn); vs_l.append(vn); ss_l.append(sn)
    return (y_p.reshape(n_p, seq_p, d), y_s.reshape(n_s, seq_s, d), jnp.stack(kp_l), jnp.stack(vp_l),
            jnp.stack(sp_l), jnp.stack(ks_l), jnp.stack(vs_l), jnp.stack(ss_l))
```

```python
import functools
import math

import jax
import jax.numpy as jnp
from jax import lax
from jax.experimental import pallas as pl
from jax.experimental.pallas import tpu as pltpu

TOP_K = 4
SWIGLU_LIMIT = 7.0
SWIGLU_ALPHA = 1.702
RMS_EPS = 1e-5
GN_EPS = 1e-6
ROPE_BASE = 10000.0
RET_CHUNK = 128

LANES = 128
MXU_DIM = 256
MIB = 1024 * 1024
HIGHEST = lax.Precision.HIGHEST
F32 = jnp.float32
BF16 = jnp.bfloat16
NEG_BIG = -1e30


def _pick(n, pref, mult=8):
    t = min(pref, n)
    t -= t % mult
    while t >= mult:
        if n % t == 0:
            return t
        t -= mult
    return n


def _params(n_grid, vmem_mib):
    return pltpu.CompilerParams(dimension_semantics=("arbitrary",) * n_grid,
                                vmem_limit_bytes=int(vmem_mib * MIB))


def _sigmoid(x):
    return 1.0 / (1.0 + jnp.exp(-x))


def _softplus(z):
    return jnp.maximum(z, 0.0) + jnp.log(1.0 + jnp.exp(-jnp.abs(z)))


def _dot_nt(a, b, **kw):
    return lax.dot_general(a, b, (((1,), (1,)), ((), ())), preferred_element_type=F32, **kw)


def _dot_tn(a, b, **kw):
    return lax.dot_general(a, b, (((0,), (0,)), ((), ())), preferred_element_type=F32, **kw)


def _rmsnorm_body(xa_ref, xb_ref, g_ref, o_ref, *, n_first):
    def norm(x_ref):
        x = x_ref[...]
        ms = jnp.mean(x * x, axis=-1, keepdims=True)
        o_ref[...] = (x * lax.rsqrt(ms + RMS_EPS) * g_ref[...]).astype(o_ref.dtype)

    @pl.when(pl.program_id(0) < n_first)
    def _():
        norm(xa_ref)

    @pl.when(pl.program_id(0) >= n_first)
    def _():
        norm(xb_ref)


def _rmsnorm(xa, xb, gain, out_dtype):
    (ta, d), tb = xa.shape, xb.shape[0]
    tm = _pick(math.gcd(ta, tb), 256)
    n_first = ta // tm
    return pl.pallas_call(
        functools.partial(_rmsnorm_body, n_first=n_first),
        grid=((ta + tb) // tm,),
        in_specs=[pl.BlockSpec((tm, d), lambda i: (jnp.minimum(i, n_first - 1), 0)),
                  pl.BlockSpec((tm, d), lambda i: (jnp.maximum(i - n_first, 0), 0)),
                  pl.BlockSpec((1, d), lambda i: (0, 0))],
        out_specs=pl.BlockSpec((tm, d), lambda i: (i, 0)),
        out_shape=jax.ShapeDtypeStruct((ta + tb, d), out_dtype),
        compiler_params=_params(1, 32),
        name="rmsnorm",
    )(xa, xb, gain.reshape(1, d))


def _mm_body(a_ref, w_ref, o_ref, wb_ref):
    @pl.when(pl.program_id(1) == 0)
    def _():
        wb_ref[...] = w_ref[...].astype(BF16)

    o_ref[...] = jnp.dot(a_ref[...], wb_ref[...], preferred_element_type=F32)


def _mm_res_body(a_ref, w_ref, ra_ref, rb_ref, o_ref, wb_ref, *, n_first):
    i = pl.program_id(1)

    @pl.when(i == 0)
    def _():
        wb_ref[...] = w_ref[...].astype(BF16)

    acc = jnp.dot(a_ref[...], wb_ref[...], preferred_element_type=F32)

    @pl.when(i < n_first)
    def _():
        o_ref[...] = ra_ref[...] + acc

    @pl.when(i >= n_first)
    def _():
        o_ref[...] = rb_ref[...] + acc


def _matmul(a, w, residual=None, name="matmul"):
    m, k = a.shape
    n = w.shape[1]
    tm, tn = _pick(m, 512), _pick(n, 512, LANES)
    in_specs = [pl.BlockSpec((tm, k), lambda j, i: (i, 0)),
                pl.BlockSpec((k, tn), lambda j, i: (0, j))]
    args = [a, w]
    body = _mm_body
    if residual is not None:
        ra, rb = residual
        tm = _pick(math.gcd(ra.shape[0], rb.shape[0]), 512)
        n_first = ra.shape[0] // tm
        assert ra.shape[0] + rb.shape[0] == m
        in_specs = [pl.BlockSpec((tm, k), lambda j, i: (i, 0)),
                    pl.BlockSpec((k, tn), lambda j, i: (0, j)),
                    pl.BlockSpec((tm, tn), lambda j, i: (jnp.minimum(i, n_first - 1), j)),
                    pl.BlockSpec((tm, tn), lambda j, i: (jnp.maximum(i - n_first, 0), j))]
        args += [ra, rb]
        body = functools.partial(_mm_res_body, n_first=n_first)
    return pl.pallas_call(
        body,
        grid=(n // tn, m // tm),
        in_specs=in_specs,
        out_specs=pl.BlockSpec((tm, tn), lambda j, i: (i, j)),
        out_shape=jax.ShapeDtypeStruct((m, n), F32),
        scratch_shapes=[pltpu.VMEM((k, tn), BF16)],
        compiler_params=_params(2, 44),
        name=name,
    )(*args)


def _branch_body(a0a_ref, a0b_ref, a1a_ref, a1b_ref, w0_ref, w1_ref, ga_ref, gb_ref, o_ref, w0b_ref, w1b_ref,
                 *, n_first):
    i = pl.program_id(1)

    @pl.when(i == 0)
    def _():
        w0b_ref[...] = w0_ref[...].astype(BF16)
        w1b_ref[...] = w1_ref[...].astype(BF16)

    def merge(a0_ref, a1_ref):
        p0 = jnp.dot(a0_ref[...], w0b_ref[...], preferred_element_type=F32)
        p1 = jnp.dot(a1_ref[...], w1b_ref[...], preferred_element_type=F32)
        o_ref[...] = (_sigmoid(ga_ref[...]) * p0 + _sigmoid(gb_ref[...]) * p1).astype(o_ref.dtype)

    @pl.when(i < n_first)
    def _():
        merge(a0a_ref, a1a_ref)

    @pl.when(i >= n_first)
    def _():
        merge(a0b_ref, a1b_ref)


def _branch_merge(o_sb, o_r, w_branch, proj, d_model):
    (ta, w), tb = o_sb[0].shape, o_sb[1].shape[0]
    t = ta + tb
    tm, tn = _pick(math.gcd(ta, tb), 512), _pick(d_model, 512, LANES)
    n_first = ta // tm
    ga_blk = 3 * d_model // tn
    gb_blk = 4 * d_model // tn
    first = lambda j, i: (jnp.minimum(i, n_first - 1), 0)
    second = lambda j, i: (jnp.maximum(i - n_first, 0), 0)
    return pl.pallas_call(
        functools.partial(_branch_body, n_first=n_first),
        grid=(d_model // tn, t // tm),
        in_specs=[pl.BlockSpec((tm, w), first),
                  pl.BlockSpec((tm, w), second),
                  pl.BlockSpec((tm, w), first),
                  pl.BlockSpec((tm, w), second),
                  pl.BlockSpec((None, w, tn), lambda j, i: (0, 0, j)),
                  pl.BlockSpec((None, w, tn), lambda j, i: (1, 0, j)),
                  pl.BlockSpec((tm, tn), lambda j, i: (i, ga_blk + j)),
                  pl.BlockSpec((tm, tn), lambda j, i: (i, gb_blk + j))],
        out_specs=pl.BlockSpec((tm, tn), lambda j, i: (i, j)),
        out_shape=jax.ShapeDtypeStruct((t, d_model), BF16),
        scratch_shapes=[pltpu.VMEM((w, tn), BF16), pltpu.VMEM((w, tn), BF16)],
        compiler_params=_params(2, 44),
        name="branch_merge",
    )(o_sb[0], o_sb[1], o_r[0], o_r[1], w_branch, w_branch, proj, proj)


def _reverse_cumsum(sp, tri):
    hi = sp.astype(BF16)
    lo = (sp - hi.astype(F32)).astype(BF16)
    return (jnp.dot(hi, tri, preferred_element_type=F32) + jnp.dot(lo, tri, preferred_element_type=F32))


def _sb_prompt_body(qi_ref, kb_ref, bias_ref, q_ref, k_ref, v_ref, tri_ref, o_ref, acc_ref, carry_ref,
                    *, blk, sub, scale):
    h = pl.program_id(1)
    p = pl.program_id(2)
    qi = qi_ref[p]
    kb = kb_ref[p]

    def step(diagonal):
        q = q_ref[...].astype(BF16)
        k = k_ref[...].astype(BF16)
        z = _dot_nt(q, k) * scale + bias_ref[h]
        sp = _softplus(z)
        if diagonal:
            visible = (lax.broadcasted_iota(jnp.int32, (blk, blk), 1)
                       < lax.broadcasted_iota(jnp.int32, (blk, blk), 0))
            sp = jnp.where(visible, sp, 0.0)
        tri = tri_ref[...]
        carry = carry_ref[...]
        weights = [None] * (blk // sub)
        for c in reversed(range(blk // sub)):
            cols = slice(c * sub, (c + 1) * sub)
            cs = _reverse_cumsum(sp[:, cols], tri)
            arg = z[:, cols] - cs - carry
            if diagonal:
                arg = jnp.where(visible[:, cols], arg, NEG_BIG)
            weights[c] = jnp.exp(arg).astype(BF16)
            carry = carry + cs[:, 0:1]
        carry_ref[...] = carry
        w = jnp.concatenate(weights, axis=1)
        acc_ref[...] += jnp.dot(w, v_ref[...].astype(BF16), preferred_element_type=F32)

    @pl.when(kb == qi)
    def _():
        acc_ref[...] = jnp.zeros_like(acc_ref)
        carry_ref[...] = jnp.zeros_like(carry_ref)
        step(True)

    @pl.when(kb != qi)
    def _():
        step(False)

    @pl.when(kb == 0)
    def _():
        o_ref[...] = acc_ref[...].astype(o_ref.dtype)


def _sb_prompt(proj, sb_bias, n_seq, seq, n_heads, dh):
    blk = _pick(seq, 512, LANES)
    sub = _pick(blk, MXU_DIM, LANES)
    nq = seq // blk
    pairs = [(qi, kb) for qi in range(nq) for kb in range(qi, -1, -1)]
    qi_tab = jnp.array([p[0] for p in pairs], jnp.int32)
    kb_tab = jnp.array([p[1] for p in pairs], jnp.int32)
    tri = jnp.tril(jnp.ones((sub, sub), F32)).astype(BF16)
    grid_spec = pltpu.PrefetchScalarGridSpec(
        num_scalar_prefetch=2,
        grid=(n_seq, n_heads, len(pairs)),
        in_specs=[pl.BlockSpec(memory_space=pltpu.SMEM),
                  pl.BlockSpec((blk, dh), lambda b, h, p, qi, kb: (b * nq + qi[p], h)),
                  pl.BlockSpec((blk, dh), lambda b, h, p, qi, kb: (b * nq + kb[p], n_heads + h)),
                  pl.BlockSpec((blk, dh), lambda b, h, p, qi, kb: (b * nq + kb[p], 2 * n_heads + h)),
                  pl.BlockSpec((sub, sub), lambda b, h, p, qi, kb: (0, 0))],
        out_specs=pl.BlockSpec((blk, dh), lambda b, h, p, qi, kb: (b * nq + qi[p], h)),
        scratch_shapes=[pltpu.VMEM((blk, dh), F32), pltpu.VMEM((blk, 1), F32)],
    )
    return pl.pallas_call(
        functools.partial(_sb_prompt_body, blk=blk, sub=sub, scale=dh ** -0.5),
        grid_spec=grid_spec,
        out_shape=jax.ShapeDtypeStruct((n_seq * seq, n_heads * dh), BF16),
        compiler_params=_params(3, 40),
        name="sb_prompt",
    )(qi_tab, kb_tab, sb_bias.astype(F32), proj, proj, proj, tri)


def _sb_sample_body(pt_ref, q_ref, kn_ref, vn_ref, *rest, n_heads, dh, n_new, page, scale, ppg):
    kp_refs, vp_refs = rest[:ppg], rest[ppg:2 * ppg]
    bias_ref, tri_ref, o_ref, acc_ref, carry_ref = rest[2 * ppg:]
    p = pl.program_id(1)
    rows = n_heads * n_new
    bias = bias_ref[...]
    tri = tri_ref[...]

    def attend(get_k, get_v, masked, acc, carry):
        z = jnp.concatenate(
            [_dot_nt(q_ref[:, h * dh:(h + 1) * dh].astype(BF16), get_k(h)) for h in range(n_heads)], axis=0)
        z = z * scale + bias
        sp = _softplus(z)
        if masked:
            t = lax.broadcasted_iota(jnp.int32, (rows, page), 0) % n_new
            s = lax.broadcasted_iota(jnp.int32, (rows, page), 1)
            visible = s < t
            sp = jnp.where(visible, sp, 0.0)
        cs = _reverse_cumsum(sp, tri)
        arg = z - cs - carry
        if masked:
            arg = jnp.where(visible, arg, NEG_BIG)
        w = jnp.exp(arg).astype(BF16)
        out = jnp.concatenate(
            [jnp.dot(w[h * n_new:(h + 1) * n_new, :], get_v(h), preferred_element_type=F32)
             for h in range(n_heads)], axis=0)
        return acc + out, carry + cs[:, 0:1]

    @pl.when(p == 0)
    def _():
        pad = jnp.zeros((page - n_new, dh), BF16)

        def new_rows(ref, h):
            return jnp.concatenate([ref[:, h * dh:(h + 1) * dh].astype(BF16), pad], axis=0)

        acc, carry = attend(functools.partial(new_rows, kn_ref), functools.partial(new_rows, vn_ref), True,
                            jnp.zeros(acc_ref.shape, F32), jnp.zeros(carry_ref.shape, F32))
        acc_ref[...] = acc
        carry_ref[...] = carry

    @pl.when(p > 0)
    def _():
        def past_rows(ref, h):
            return ref[pl.ds(h, page, stride=n_heads), :].astype(BF16)

        acc, carry = acc_ref[...], carry_ref[...]
        for u in range(ppg):
            acc, carry = attend(functools.partial(past_rows, kp_refs[u]), functools.partial(past_rows, vp_refs[u]),
                                False, acc, carry)
        acc_ref[...] = acc
        carry_ref[...] = carry

    @pl.when(p == pl.num_programs(1) - 1)
    def _():
        for h in range(n_heads):
            o_ref[:, h * dh:(h + 1) * dh] = acc_ref[h * n_new:(h + 1) * n_new, :].astype(o_ref.dtype)


def _sb_sample(proj, row0, cache_k, cache_v, page_table, sb_bias, n_seq, n_new, n_heads, dh):
    n_pool, page = cache_k.shape[0], cache_k.shape[1]
    n_pages = page_table.shape[1]
    width = n_heads * dh
    ppg = _pick(n_pages, 4, 1)
    assert row0 % n_new == 0 and n_new % 8 == 0 and page == LANES
    blk0 = row0 // n_new
    kc = cache_k.reshape(n_pool, page * n_heads, dh)
    vc = cache_v.reshape(n_pool, page * n_heads, dh)
    bias_rows = jnp.repeat(sb_bias.astype(F32), n_new).reshape(n_heads * n_new, 1)
    tri = jnp.tril(jnp.ones((page, page), F32)).astype(BF16)

    def page_idx(u, b, p, pt):
        return (pt[b * n_pages + n_pages - (jnp.maximum(p, 1) - 1) * ppg - 1 - u], 0, 0)

    page_specs = [pl.BlockSpec((None, page * n_heads, dh), functools.partial(page_idx, u)) for u in range(ppg)]
    grid_spec = pltpu.PrefetchScalarGridSpec(
        num_scalar_prefetch=1,
        grid=(n_seq, n_pages // ppg + 1),
        in_specs=[pl.BlockSpec((n_new, width), lambda b, p, pt: (blk0 + b, 0)),
                  pl.BlockSpec((n_new, width), lambda b, p, pt: (blk0 + b, 1)),
                  pl.BlockSpec((n_new, width), lambda b, p, pt: (blk0 + b, 2))]
        + page_specs + page_specs
        + [pl.BlockSpec((n_heads * n_new, 1), lambda b, p, pt: (0, 0)),
           pl.BlockSpec((page, page), lambda b, p, pt: (0, 0))],
        out_specs=pl.BlockSpec((n_new, width), lambda b, p, pt: (b, 0)),
        scratch_shapes=[pltpu.VMEM((n_heads * n_new, dh), F32), pltpu.VMEM((n_heads * n_new, 1), F32)],
    )
    return pl.pallas_call(
        functools.partial(_sb_sample_body, n_heads=n_heads, dh=dh, n_new=n_new, page=page, scale=dh ** -0.5,
                          ppg=ppg),
        grid_spec=grid_spec,
        out_shape=jax.ShapeDtypeStruct((n_seq * n_new, width), BF16),
        compiler_params=_params(2, 40),
        name="sb_sample",
    )(page_table.reshape(-1).astype(jnp.int32), proj, proj, proj, *([kc] * ppg), *([vc] * ppg), bias_rows, tri)


def _retention_body(q_ref, k_ref, v_ref, g_ref, cos_ref, sin_ref, gain_ref, s0_ref, o_ref, sout_ref,
                    state_ref, *, n_heads, dk, dv, chunk):
    c = pl.program_id(1)

    @pl.when(c == 0)
    def _():
        state_ref[...] = s0_ref[...]

    cosf = cos_ref[...]
    sinf = sin_ref[...]
    ri = lax.broadcasted_iota(jnp.int32, (chunk, chunk), 0)
    ci = lax.broadcasted_iota(jnp.int32, (chunk, chunk), 1)
    diff = (ri - ci).astype(F32)
    idx = lax.broadcasted_iota(jnp.int32, (chunk, 1), 0).astype(F32)
    for h in range(n_heads):
        log_gamma = math.log(1.0 - 2.0 ** (-5.0 - h))
        q = q_ref[:, h * dk:(h + 1) * dk]
        k = k_ref[:, h * dk:(h + 1) * dk]
        v = v_ref[:, h * dv:(h + 1) * dv]
        q = q * cosf + pltpu.roll(q, dk // 2, axis=1) * sinf
        k = (k * cosf + pltpu.roll(k, dk // 2, axis=1) * sinf) * (dk ** -0.5)
        state = state_ref[h]
        intra = jnp.where(diff >= 0, jnp.exp(log_gamma * jnp.maximum(diff, 0.0)), 0.0)
        vb = v.astype(BF16)
        qb = q.astype(BF16)
        scores = _dot_nt(qb, k.astype(BF16)) * intra
        inner = jnp.dot(scores.astype(BF16), vb, preferred_element_type=F32)
        cross = jnp.dot(qb, state.astype(BF16), preferred_element_type=F32) * jnp.exp(log_gamma * (idx + 1.0))
        k_dec = k * jnp.exp(log_gamma * (chunk - 1.0 - idx))
        state_ref[h] = state * math.exp(log_gamma * chunk) + _dot_tn(k_dec.astype(BF16), vb)
        o = inner + cross
        mu = jnp.mean(o, axis=-1, keepdims=True)
        var = jnp.mean(jnp.square(o - mu), axis=-1, keepdims=True)
        o = (o - mu) * lax.rsqrt(var + GN_EPS) * gain_ref[h:h + 1, :]
        g = g_ref[:, h * dv:(h + 1) * dv]
        o_ref[:, h * dv:(h + 1) * dv] = (g * _sigmoid(g) * o).astype(o_ref.dtype)

    @pl.when(c == pl.num_programs(1) - 1)
    def _():
        sout_ref[...] = state_ref[...]


def _retention(proj, row0, n_seq, seq, pos0, state0, gn_gain, width):
    n_heads, dk, dv = state0.shape[1], state0.shape[2], state0.shape[3]
    chunk = RET_CHUNK if seq % RET_CHUNK == 0 else seq
    n_chunks = seq // chunk
    qk_w = n_heads * dk
    assert n_heads * dv == width and 2 * qk_w == width and row0 % chunk == 0
    blk0 = row0 // chunk
    half = dk // 2
    inv_freq = ROPE_BASE ** (-jnp.arange(half, dtype=F32) / half)
    ang = (pos0 + jnp.arange(seq, dtype=jnp.int32)).astype(F32)[:, None] * inv_freq[None, :]
    cos_t = jnp.concatenate([jnp.cos(ang), jnp.cos(ang)], axis=1)
    sin_t = jnp.concatenate([-jnp.sin(ang), jnp.sin(ang)], axis=1)
    row = lambda b, c: blk0 + b * n_chunks + c
    return pl.pallas_call(
        functools.partial(_retention_body, n_heads=n_heads, dk=dk, dv=dv, chunk=chunk),
        grid=(n_seq, n_chunks),
        in_specs=[pl.BlockSpec((chunk, qk_w), lambda b, c: (row(b, c), 6)),
                  pl.BlockSpec((chunk, qk_w), lambda b, c: (row(b, c), 7)),
                  pl.BlockSpec((chunk, width), lambda b, c: (row(b, c), 4)),
                  pl.BlockSpec((chunk, width), lambda b, c: (row(b, c), 5)),
                  pl.BlockSpec((chunk, dk), lambda b, c: (c, 0)),
                  pl.BlockSpec((chunk, dk), lambda b, c: (c, 0)),
                  pl.BlockSpec((n_heads, dv), lambda b, c: (0, 0)),
                  pl.BlockSpec((None, n_heads, dk, dv), lambda b, c: (b, 0, 0, 0))],
        out_specs=[pl.BlockSpec((chunk, width), lambda b, c: (b * n_chunks + c, 0)),
                   pl.BlockSpec((None, n_heads, dk, dv), lambda b, c: (b, 0, 0, 0))],
        out_shape=[jax.ShapeDtypeStruct((n_seq * seq, width), BF16),
                   jax.ShapeDtypeStruct(state0.shape, F32)],
        scratch_shapes=[pltpu.VMEM((n_heads, dk, dv), F32)],
        compiler_params=_params(2, 32),
        name="retention",
    )(proj, proj, proj, proj, cos_t, sin_t, gn_gain.astype(F32), state0.astype(F32))


def _router_body(h_ref, g_ref, wr_ref, br_ref, xn_ref, idx_ref, wt_ref):
    x = h_ref[...]
    ms = jnp.mean(x * x, axis=-1, keepdims=True)
    xn = x * lax.rsqrt(ms + RMS_EPS) * g_ref[...]
    xn_ref[...] = xn
    logits = jnp.dot(xn, wr_ref[...], precision=HIGHEST, preferred_element_type=F32) + br_ref[...]
    lane = lax.broadcasted_iota(jnp.int32, logits.shape, 1)
    vals, idxs = [], []
    work = logits
    for _ in range(TOP_K):
        m = jnp.max(work, axis=-1, keepdims=True)
        i = jnp.min(jnp.where(work == m, lane, LANES), axis=-1, keepdims=True)
        vals.append(m)
        idxs.append(i)
        work = jnp.where(lane == i, -jnp.inf, work)
    exps = [jnp.exp(v - vals[0]) for v in vals]
    denom = exps[0] + exps[1] + exps[2] + exps[3]
    idx_out = jnp.zeros(logits.shape, jnp.int32)
    wt_out = jnp.zeros(logits.shape, F32)
    for k in range(TOP_K):
        idx_out = jnp.where(lane == k, idxs[k], idx_out)
        wt_out = jnp.where(lane == k, exps[k] / denom, wt_out)
    idx_ref[...] = idx_out
    wt_ref[...] = wt_out


def _router(h, gain, w_router, b_router):
    t, d = h.shape
    e = w_router.shape[1]
    assert e <= LANES
    tm = _pick(t, 256)
    wr = jnp.zeros((d, LANES), F32).at[:, :e].set(w_router)
    br = jnp.full((1, LANES), NEG_BIG, F32).at[0, :e].set(b_router)
    return pl.pallas_call(
        _router_body,
        grid=(t // tm,),
        in_specs=[pl.BlockSpec((tm, d), lambda i: (i, 0)),
                  pl.BlockSpec((1, d), lambda i: (0, 0)),
                  pl.BlockSpec((d, LANES), lambda i: (0, 0)),
                  pl.BlockSpec((1, LANES), lambda i: (0, 0))],
        out_specs=[pl.BlockSpec((tm, d), lambda i: (i, 0)),
                   pl.BlockSpec((tm, LANES), lambda i: (i, 0)),
                   pl.BlockSpec((tm, LANES), lambda i: (i, 0))],
        out_shape=[jax.ShapeDtypeStruct((t, d), F32),
                   jax.ShapeDtypeStruct((t, LANES), jnp.int32),
                   jax.ShapeDtypeStruct((t, LANES), F32)],
        compiler_params=_params(1, 40),
        name="router",
    )(h, gain.reshape(1, d), wr, br)


def _row_copy(src_hbm, src_row, dst, dst_row, sem):
    return pltpu.make_async_copy(src_hbm.at[pl.ds(src_row, 1)], dst.at[pl.ds(dst_row, 1)], sem)


def _dispatch_body(src_ref, used_ref, x_hbm, o_ref, buf_ref, sem, *, rows):
    i = pl.program_id(0)
    slot = i % 2

    def issue(tile, to_slot):
        def body(r, carry):
            _row_copy(x_hbm, src_ref[tile * rows + r], buf_ref.at[to_slot], r, sem.at[to_slot]).start()
            return carry

        lax.fori_loop(0, rows, body, 0, unroll=8)

    last = pl.num_programs(0) - 1

    @pl.when(jnp.logical_and(i == 0, used_ref[0] == 1))
    def _():
        issue(0, 0)

    @pl.when(jnp.logical_and(i < last, used_ref[jnp.minimum(i + 1, last)] == 1))
    def _():
        issue(i + 1, 1 - slot)

    @pl.when(used_ref[i] == 1)
    def _():
        def drain(r, carry):
            _row_copy(x_hbm, 0, buf_ref.at[slot], r, sem.at[slot]).wait()
            return carry

        lax.fori_loop(0, rows, drain, 0, unroll=8)
        o_ref[...] = buf_ref[slot].astype(o_ref.dtype)

    @pl.when(used_ref[i] == 0)
    def _():
        o_ref[...] = jnp.zeros_like(o_ref)


def _dispatch(xn, src_tok, tile_used, rows_pad, rows):
    d = xn.shape[1]
    grid_spec = pltpu.PrefetchScalarGridSpec(
        num_scalar_prefetch=2,
        grid=(rows_pad // rows,),
        in_specs=[pl.BlockSpec(memory_space=pl.ANY)],
        out_specs=pl.BlockSpec((rows, d), lambda i, s, u: (i, 0)),
        scratch_shapes=[pltpu.VMEM((2, rows, d), F32), pltpu.SemaphoreType.DMA((2,))],
    )
    return pl.pallas_call(
        functools.partial(_dispatch_body, rows=rows),
        grid_spec=grid_spec,
        out_shape=jax.ShapeDtypeStruct((rows_pad, d), BF16),
        compiler_params=_params(1, 32),
        name="moe_dispatch",
    )(src_tok, tile_used, xn)


def _combine_body(pos_ref, h_ref, wt_ref, ys_hbm, g_ref, o_ref, buf_ref, sem, *, rows, row0, final_norm):
    i = pl.program_id(0)
    slot = i % 2

    def issue(tile, to_slot):
        base = (row0 + tile * rows) * TOP_K

        def body(r, carry):
            for k in range(TOP_K):
                _row_copy(ys_hbm, pos_ref[base + r * TOP_K + k], buf_ref.at[to_slot, k], r, sem.at[to_slot]).start()
            return carry

        lax.fori_loop(0, rows, body, 0, unroll=2)

    @pl.when(i == 0)
    def _():
        issue(0, 0)

    @pl.when(i + 1 < pl.num_programs(0))
    def _():
        issue(i + 1, 1 - slot)

    def drain(r, carry):
        for k in range(TOP_K):
            _row_copy(ys_hbm, 0, buf_ref.at[slot, k], r, sem.at[slot]).wait()
        return carry

    lax.fori_loop(0, rows, drain, 0, unroll=2)
    wt = wt_ref[...]
    moe = ((wt[:, 0:1] * buf_ref[slot, 0] + wt[:, 1:2] * buf_ref[slot, 1])
           + (wt[:, 2:3] * buf_ref[slot, 2] + wt[:, 3:4] * buf_ref[slot, 3]))
    y = h_ref[...] + moe
    if final_norm:
        ms = jnp.mean(y * y, axis=-1, keepdims=True)
        y = y * lax.rsqrt(ms + RMS_EPS) * g_ref[...]
    o_ref[...] = y


def _combine(h, wt, ys, pos, gain, row0, n_rows, final_norm):
    d = h.shape[1]
    rows = _pick(n_rows, 64)
    assert row0 % rows == 0
    blk0 = row0 // rows
    grid_spec = pltpu.PrefetchScalarGridSpec(
        num_scalar_prefetch=1,
        grid=(n_rows // rows,),
        in_specs=[pl.BlockSpec((rows, d), lambda i, s: (blk0 + i, 0)),
                  pl.BlockSpec((rows, LANES), lambda i, s: (blk0 + i, 0)),
                  pl.BlockSpec(memory_space=pl.ANY),
                  pl.BlockSpec((1, d), lambda i, s: (0, 0))],
        out_specs=pl.BlockSpec((rows, d), lambda i, s: (i, 0)),
        scratch_shapes=[pltpu.VMEM((2, TOP_K, rows, d), F32), pltpu.SemaphoreType.DMA((2,))],
    )
    return pl.pallas_call(
        functools.partial(_combine_body, rows=rows, row0=row0, final_norm=final_norm),
        grid_spec=grid_spec,
        out_shape=jax.ShapeDtypeStruct((n_rows, d), F32),
        compiler_params=_params(1, 32),
        name="moe_combine",
    )(pos, h, wt, ys, gain.reshape(1, d))


def _swiglu_compact(gu, sel, tn):
    nxt = pltpu.roll(gu, tn - 1, axis=1)
    glu = jnp.minimum(gu, SWIGLU_LIMIT)
    lin = jnp.clip(nxt, -SWIGLU_LIMIT, SWIGLU_LIMIT)
    hdn = glu * _sigmoid(SWIGLU_ALPHA * glu) * (lin + 1.0)
    even = lax.broadcasted_iota(jnp.int32, hdn.shape, 1) % 2 == 0
    hdn = jnp.where(even, hdn, 0.0).astype(BF16)
    width = sel.shape[0]
    return jnp.concatenate(
        [jnp.dot(hdn[:, c * width:(c + 1) * width], sel, preferred_element_type=F32).astype(BF16)
         for c in range(tn // width)], axis=1)


def _expert_up_body(be_ref, used_ref, src_ref, x_ref, w_ref, b_ref, sel_ref, o_ref, wb_ref, *, tm, tn):
    @pl.when(used_ref[pl.program_id(0)] == 1)
    def _():
        wb_ref[...] = w_ref[...].astype(BF16)
        for g in range(x_ref.shape[0] // tm):
            rows = slice(g * tm, (g + 1) * tm)
            gu = jnp.dot(x_ref[rows, :], wb_ref[...], preferred_element_type=F32) + b_ref[...]
            o_ref[rows, :] = _swiglu_compact(gu, sel_ref[...], tn)

    @pl.when(used_ref[pl.program_id(0)] == 0)
    def _():
        o_ref[...] = jnp.zeros_like(o_ref)


def _expert_down_body(be_ref, used_ref, src_ref, x_ref, w_ref, b_ref, o_ref, wb_ref, *, tm):
    @pl.when(used_ref[pl.program_id(0)] == 1)
    def _():
        wb_ref[...] = w_ref[...].astype(BF16)
        for g in range(x_ref.shape[0] // tm):
            rows = slice(g * tm, (g + 1) * tm)
            o_ref[rows, :] = jnp.dot(x_ref[rows, :], wb_ref[...], preferred_element_type=F32) + b_ref[...]

    @pl.when(used_ref[pl.program_id(0)] == 0)
    def _():
        o_ref[...] = jnp.zeros_like(o_ref)


def _expert_matmul(body, name, xs, w, b, extra, out_cols, out_dtype, tabs, rb, tn):
    block_e, block_used, block_src = tabs
    n_exp, k, n = w.shape
    n_j = n // tn
    tn_out = tn * out_cols // n

    def x_idx(s, j, be, used, src):
        return (src[s], 0)

    def w_idx(s, j, be, used, src):
        return (be[s], 0, jnp.where(used[s] == 1, j, n_j - 1))

    def o_idx(s, j, be, used, src):
        return (s, j)

    grid_spec = pltpu.PrefetchScalarGridSpec(
        num_scalar_prefetch=3,
        grid=(block_e.shape[0], n_j),
        in_specs=[pl.BlockSpec((rb, k), x_idx),
                  pl.BlockSpec((None, k, tn), w_idx),
                  pl.BlockSpec((None, 1, tn), w_idx)]
        + [pl.BlockSpec(e.shape, lambda s, j, be, used, src: (0, 0)) for e in extra],
        out_specs=pl.BlockSpec((rb, tn_out), o_idx),
        scratch_shapes=[pltpu.VMEM((k, tn), BF16)],
    )
    return pl.pallas_call(
        body,
        grid_spec=grid_spec,
        out_shape=jax.ShapeDtypeStruct((xs.shape[0], out_cols), out_dtype),
        compiler_params=_params(2, 56),
        name=name,
    )(block_e, block_used, block_src, xs, w, b.reshape(n_exp, 1, n), *extra)


def _expert_up(xs, w_gate_up, b_gate_up, tabs, rb, tm):
    n2 = w_gate_up.shape[2]
    tn = _pick(n2, 512, MXU_DIM)
    sel = (jnp.arange(MXU_DIM)[:, None] == 2 * jnp.arange(MXU_DIM // 2)[None, :]).astype(BF16)
    return _expert_matmul(functools.partial(_expert_up_body, tm=tm, tn=tn), "expert_up", xs, w_gate_up, b_gate_up,
                          [sel], n2 // 2, BF16, tabs, rb, tn)


def _expert_down(hdn, w_down, b_down, tabs, rb, tm):
    d = w_down.shape[2]
    return _expert_matmul(functools.partial(_expert_down_body, tm=tm), "expert_down", hdn, w_down, b_down,
                          [], d, F32, tabs, rb, _pick(d, 512, LANES))


def _lookup(table, idx):
    hit = idx[:, None] == jnp.arange(table.shape[0], dtype=jnp.int32)[None, :]
    return jnp.sum(jnp.where(hit, table[None, :], 0), axis=1)


def _routing_tables(top_idx, n_exp, tm, rb):
    n_assign = top_idx.size
    n_blocks = -(-(n_assign + n_exp * (rb - 1)) // rb)
    e_flat = top_idx.reshape(-1)
    lanes = jnp.arange(n_exp, dtype=jnp.int32)
    onehot = e_flat[:, None] == lanes[None, :]
    chunk = _pick(n_assign, 512)
    oh = onehot.astype(BF16).reshape(n_assign // chunk, chunk, n_exp)
    tri = jnp.tril(jnp.ones((chunk, chunk), F32)).astype(BF16)
    within = jnp.einsum("ij,bje->bie", tri, oh, preferred_element_type=F32)
    totals = within[:, -1, :].astype(jnp.int32)
    before = jnp.cumsum(totals, axis=0) - totals
    csum = within.astype(jnp.int32) + before[:, None, :]
    sizes = before[-1] + totals[-1]
    padded = (sizes + rb - 1) // rb * rb
    ends = jnp.cumsum(padded)
    starts = ends - padded
    dest = jnp.sum(jnp.where(onehot, csum.reshape(n_assign, n_exp) - 1 + starts[None, :], 0), axis=1)
    dest = dest.astype(jnp.int32)
    rows_pad = n_blocks * rb
    src_tok = jnp.zeros((rows_pad,), jnp.int32).at[dest].set(jnp.arange(n_assign, dtype=jnp.int32) // TOP_K)
    block_ids = jnp.arange(n_blocks, dtype=jnp.int32)
    n_used = ends[-1] // rb
    block_used = (block_ids < n_used).astype(jnp.int32)
    block_src = jnp.minimum(block_ids, n_used - 1)
    block_e = jnp.minimum(jnp.sum((ends[None, :] <= (block_src * rb)[:, None]).astype(jnp.int32), axis=1), n_exp - 1)
    n_tiles = rows_pad // tm
    tile_start = jnp.arange(n_tiles, dtype=jnp.int32) * tm
    tile_blk = jnp.minimum(tile_start // rb, n_blocks - 1)
    tile_e = _lookup(block_e, tile_blk)
    real_end = _lookup(starts + sizes, tile_e)
    tile_used = ((tile_start < real_end) & (_lookup(block_used, tile_blk) == 1)
                 & (tile_start < n_blocks * rb)).astype(jnp.int32)
    return dest, src_tok, (block_e.astype(jnp.int32), block_used, block_src), tile_used, rows_pad


def _layer(x_p, x_s, n_p, seq_p, n_s, seq_s, past_len, cache_k, cache_v, state_s, page_table, w, last, final_gain):
    (norm_mix_gain, w_in, sb_bias, ret_gn_gain, w_branch, w_out, norm_ffn_gain, w_router, b_router,
     w_gate_up, b_gate_up, w_down, b_down) = w
    d = x_p.shape[1]
    n_heads_sb, dh = cache_k.shape[2], cache_k.shape[3]
    width = n_heads_sb * dh
    t_p = n_p * seq_p
    t_s = n_s * seq_s
    assert 2 * width == d

    proj = _matmul(_rmsnorm(x_p, x_s, norm_mix_gain, BF16), w_in, name="in_proj")

    o_sb = (_sb_prompt(proj, sb_bias, n_p, seq_p, n_heads_sb, dh),
            _sb_sample(proj, t_p, cache_k, cache_v, page_table, sb_bias, n_s, seq_s, n_heads_sb, dh))
    zero_state = jnp.zeros((n_p,) + state_s.shape[1:], F32)
    o_r_p, state_p = _retention(proj, 0, n_p, seq_p, 0, zero_state, ret_gn_gain, width)
    o_r_s, state_s_new = _retention(proj, t_p, n_s, seq_s, past_len, state_s, ret_gn_gain, width)

    merged = _branch_merge(o_sb, (o_r_p, o_r_s), w_branch, proj, d)
    h = _matmul(merged, w_out, residual=(x_p, x_s), name="out_proj")

    xn, idx_l, wt_l = _router(h, norm_ffn_gain, w_router, b_router)
    n_exp = w_router.shape[1]
    tm = MXU_DIM
    rb = tm * -(-11 * TOP_K * h.shape[0] // (10 * n_exp * tm))
    dest, src_tok, tabs, tile_used, rows_pad = _routing_tables(idx_l[:, :TOP_K], n_exp, tm, rb)
    xs = _dispatch(xn, src_tok, tile_used, rows_pad, tm)
    hdn = _expert_up(xs, w_gate_up, b_gate_up, tabs, rb, tm)
    ys = _expert_down(hdn, w_down, b_down, tabs, rb, tm)
    y_p = _combine(h, wt_l, ys, dest, final_gain, 0, t_p, last)
    y_s = _combine(h, wt_l, ys, dest, final_gain, t_p, t_s, last)

    k_cols = proj[:, width:2 * width]
    v_cols = proj[:, 2 * width:3 * width]
    kv = (k_cols[:t_p].reshape(n_p, seq_p, n_heads_sb, dh), v_cols[:t_p].reshape(n_p, seq_p, n_heads_sb, dh),
          k_cols[t_p:].reshape(n_s, seq_s, n_heads_sb, dh), v_cols[t_p:].reshape(n_s, seq_s, n_heads_sb, dh))
    return y_p, y_s, kv, state_p, state_s_new


def kernel(x_prompt, x_sample, cache_k, cache_v, state_ret, page_table, norm_mix_gain, w_in, sb_bias, ret_gn_gain, w_branch, w_out, norm_ffn_gain, w_router, b_router, w_gate_up, b_gate_up, w_down, b_down, norm_final_gain):
    n_p, seq_p, d = x_prompt.shape
    n_s, seq_s, _ = x_sample.shape
    depth = w_in.shape[0]
    past_len = page_table.shape[1] * cache_k.shape[2]
    y_p = x_prompt.reshape(n_p * seq_p, d)
    y_s = x_sample.reshape(n_s * seq_s, d)
    kp_l, vp_l, sp_l, ks_l, vs_l, ss_l = [], [], [], [], [], []
    for layer in range(depth):
        w = (norm_mix_gain[layer], w_in[layer], sb_bias[layer], ret_gn_gain[layer], w_branch[layer],
             w_out[layer], norm_ffn_gain[layer], w_router[layer], b_router[layer], w_gate_up[layer],
             b_gate_up[layer], w_down[layer], b_down[layer])
        y_p, y_s, (kp, vp, kn, vn), sp, sn = _layer(
            y_p, y_s, n_p, seq_p, n_s, seq_s, past_len, cache_k[layer], cache_v[layer], state_ret[layer],
            page_table, w, layer == depth - 1, norm_final_gain)
        kp_l.append(kp); vp_l.append(vp); sp_l.append(sp)
        ks_l.append(kn); vs_l.append(vn); ss_l.append(sn)
    return (y_p.reshape(n_p, seq_p, d), y_s.reshape(n_s, seq_s, d), jnp.stack(kp_l), jnp.stack(vp_l),
            jnp.stack(sp_l), jnp.stack(ks_l), jnp.stack(vs_l), jnp.stack(ss_l))
```
